```python
import math
import jax, jax.numpy as jnp
from jax import lax
import numpy as np

D_MODEL = 2048
BATCH = 8
SEQ = 2048
DEPTH = 1
DEC_BATCH = 32
DEC_SEQ = 4
PAST_LEN = 16384
PAGE_SIZE = 128

MIX_WIDTH = D_MODEL
ATTN_WIDTH = MIX_WIDTH // 2
POOL_WIDTH = MIX_WIDTH - ATTN_WIDTH
HEAD_DIM = 64
N_HEADS_A = ATTN_WIDTH // HEAD_DIM
DILATED_CONFIGS = ((128, 1), (512, 4), (2048, 16))
MAX_WINDOW = max(w for w, _ in DILATED_CONFIGS)
BLOCK = 128
POOL_WINDOWS = (2, 4, 8, 16)
N_POOL_GROUPS = len(POOL_WINDOWS)
POOL_GROUP_DIM = POOL_WIDTH // N_POOL_GROUPS
POOL_STATE = max(POOL_WINDOWS) - 1
NUM_BUCKETS = 32
MAX_DISTANCE = MAX_WINDOW
D_FF = -(-8 * D_MODEL // (3 * 256)) * 256
IN_COLS = 3 * ATTN_WIDTH + POOL_WIDTH
EPS = 1e-6

kernel_name = "hymba_dilated_pool_decoder_step"


def rmsnorm(x, g):
    xf = x.astype(jnp.float32)
    y = xf * lax.rsqrt(jnp.mean(xf * xf, axis=-1, keepdims=True) + EPS)
    return (y * g.astype(jnp.float32)).astype(x.dtype)


def t5_bucket(dist):
    max_exact = NUM_BUCKETS // 2
    df = jnp.maximum(dist, 1).astype(jnp.float32)
    large = max_exact + (jnp.log(df / max_exact) / math.log(MAX_DISTANCE / max_exact)
                         * (NUM_BUCKETS - max_exact)).astype(jnp.int32)
    large = jnp.minimum(large, NUM_BUCKETS - 1)
    return jnp.where(dist < max_exact, dist, large)


def to_strided(t, dil):
    B, S = t.shape[0], t.shape[1]
    rest = t.shape[2:]
    L = S // dil
    return t.reshape((B, L, dil) + rest).swapaxes(1, 2).reshape((B * dil, L) + rest)


def from_strided(t, B, dil, L):
    rest = t.shape[2:]
    t = t[:, :L]
    return t.reshape((B, dil, L) + rest).swapaxes(1, 2).reshape((B, dil * L) + rest)


def dilated_attention_prompt(q, k, v, rel_bias):
    B, S, H, Dh = q.shape
    scale = HEAD_DIM ** -0.5
    nums, ms, ss = [], [], []
    for window, dil in DILATED_CONFIGS:
        sub_w = window // dil
        L = S // dil
        nb = -(-L // BLOCK)
        Lp = nb * BLOCK
        Bd = B * dil
        qs = jnp.pad(to_strided(q, dil), ((0, 0), (0, Lp - L), (0, 0), (0, 0)))
        kpad = ((0, 0), (BLOCK, Lp - L), (0, 0), (0, 0))
        ks = jnp.pad(to_strided(k, dil), kpad)
        vs = jnp.pad(to_strided(v, dil), kpad)
        qb = qs.reshape(Bd, nb, BLOCK, H, Dh)
        kb = jnp.concatenate([ks[:, :Lp].reshape(Bd, nb, BLOCK, H, Dh),
                              ks[:, BLOCK:].reshape(Bd, nb, BLOCK, H, Dh)], axis=2)
        vb = jnp.concatenate([vs[:, :Lp].reshape(Bd, nb, BLOCK, H, Dh),
                              vs[:, BLOCK:].reshape(Bd, nb, BLOCK, H, Dh)], axis=2)
        qi = jnp.arange(BLOCK)
        kj = jnp.arange(2 * BLOCK)
        dist = BLOCK + qi[:, None] - kj[None, :]
        key_pos = (jnp.arange(nb)[:, None] - 1) * BLOCK + kj[None, :]
        valid = ((dist >= 0) & (dist <= sub_w))[None] & (key_pos >= 0)[:, None, :]
        bias_sub = rel_bias[t5_bucket(dil * jnp.arange(sub_w + 1))].T
        bias = bias_sub[:, jnp.clip(dist, 0, sub_w)].astype(jnp.float32)
        logits = jnp.einsum('bnqhd,bnkhd->bnhqk', qb, kb,
                            preferred_element_type=jnp.float32) * scale + bias[None, None]
        logits = jnp.where(valid[None, :, None], logits, -jnp.inf)
        m = jnp.max(logits, axis=-1)
        p = jnp.exp(logits - m[..., None])
        s = jnp.sum(p, axis=-1)
        num = jnp.einsum('bnhqk,bnkhd->bnqhd', p, vb.astype(jnp.float32))
        nums.append(from_strided(num.reshape(Bd, Lp, H, Dh), B, dil, L))
        ms.append(from_strided(m.transpose(0, 1, 3, 2).reshape(Bd, Lp, H), B, dil, L))
        ss.append(from_strided(s.transpose(0, 1, 3, 2).reshape(Bd, Lp, H), B, dil, L))
    m_all = jnp.stack(ms, 0)
    wts = jnp.exp(m_all - jnp.max(m_all, axis=0))
    num = sum(wts[i][..., None] * nums[i] for i in range(len(nums)))
    den = jnp.sum(wts * jnp.stack(ss, 0), axis=0)
    out = num / den[..., None]
    return out.reshape(B, S, H * Dh).astype(q.dtype)


def dilated_attention_sample(q, k, v, ck, cv, rel_bias):
    DB, T, H, Dh = q.shape
    WC = ck.shape[1]
    scale = HEAD_DIM ** -0.5
    dists = jnp.concatenate([dil * jnp.arange(w // dil + 1) for w, dil in DILATED_CONFIGS])
    bias = rel_bias[t5_bucket(dists)].T.astype(jnp.float32)
    kcat = jnp.concatenate([ck.astype(k.dtype), k], axis=1)
    vcat = jnp.concatenate([cv.astype(v.dtype), v], axis=1)
    idx = WC + jnp.arange(T)[:, None] - dists[None, :]
    valid = idx >= 0
    idxc = jnp.maximum(idx, 0)
    kg = kcat[:, idxc]
    vg = vcat[:, idxc]
    logits = jnp.einsum('bthd,btkhd->bhtk', q, kg,
                        preferred_element_type=jnp.float32) * scale + bias[None, :, None, :]
    logits = jnp.where(valid[None, None], logits, -jnp.inf)
    p = jax.nn.softmax(logits, axis=-1)
    out = jnp.einsum('bhtk,btkhd->bthd', p, vg.astype(jnp.float32))
    return out.reshape(DB, T, H * Dh).astype(q.dtype)


def pool_mix(ucat, n_hist, w_pool, pool_scale):
    B, R, C = ucat.shape
    uf = ucat.astype(jnp.float32)
    cs = jnp.concatenate([jnp.zeros((B, 1, C), jnp.float32), jnp.cumsum(uf, axis=1)], axis=1)
    r = jnp.arange(n_hist, R)
    outs = []
    for g, w in enumerate(POOL_WINDOWS):
        sl = slice(g * POOL_GROUP_DIM, (g + 1) * POOL_GROUP_DIM)
        lo = jnp.maximum(r + 1 - w, 0)
        cnt = (r + 1 - lo).astype(jnp.float32)
        csg = cs[..., sl]
        pooled = (csg[:, r + 1] - csg[:, lo]) / cnt[None, :, None] - uf[:, n_hist:, sl]
        outs.append(pooled @ w_pool[g].astype(jnp.float32))
    y = jnp.concatenate(outs, axis=-1) * pool_scale.astype(jnp.float32)
    return y.astype(ucat.dtype)


def mix_input(x, norm_mix, w_in):
    B, T, _ = x.shape
    proj = rmsnorm(x, norm_mix) @ w_in
    q = proj[..., :ATTN_WIDTH].reshape(B, T, N_HEADS_A, HEAD_DIM)
    k = proj[..., ATTN_WIDTH:2 * ATTN_WIDTH].reshape(B, T, N_HEADS_A, HEAD_DIM)
    v = proj[..., 2 * ATTN_WIDTH:3 * ATTN_WIDTH].reshape(B, T, N_HEADS_A, HEAD_DIM)
    u = proj[..., 3 * ATTN_WIDTH:]
    return q, k, v, u


def layer_tail(x, attn_out, pool_out, w_out, norm_ffn, w_gate, w_up, w_down):
    x = x + jnp.concatenate([attn_out, pool_out], axis=-1) @ w_out
    h = rmsnorm(x, norm_ffn)
    return x + (jax.nn.silu(h @ w_gate) * (h @ w_up)) @ w_down


def setup_inputs(seed: int = 0) -> dict:
    key = jax.random.key(seed)
    ks = jax.random.split(key, 16)
    win_cache = min(MAX_WINDOW, PAST_LEN)
    nrm = jax.random.normal
    f32 = jnp.float32
    return {
        "x_prompt": nrm(ks[0], (BATCH, SEQ, D_MODEL), f32),
        "x_sample": nrm(ks[1], (DEC_BATCH, DEC_SEQ, D_MODEL), f32),
        "cache_k": nrm(ks[2], (DEPTH, DEC_BATCH, win_cache, N_HEADS_A, HEAD_DIM), f32),
        "cache_v": nrm(ks[3], (DEPTH, DEC_BATCH, win_cache, N_HEADS_A, HEAD_DIM), f32),
        "state_pool": nrm(ks[4], (DEPTH, DEC_BATCH, POOL_STATE, POOL_WIDTH), f32),
        "rel_bias": 0.5 * nrm(ks[5], (NUM_BUCKETS, N_HEADS_A), f32),
        "norm_mix": 1.0 + 0.05 * nrm(ks[6], (DEPTH, D_MODEL), f32),
        "w_in": nrm(ks[7], (DEPTH, D_MODEL, IN_COLS), f32) * D_MODEL ** -0.5,
        "w_pool": nrm(ks[8], (DEPTH, N_POOL_GROUPS, POOL_GROUP_DIM, POOL_GROUP_DIM), f32) * POOL_GROUP_DIM ** -0.5,
        "pool_scale": 1.0 + 0.05 * nrm(ks[9], (DEPTH, POOL_WIDTH), f32),
        "w_out": nrm(ks[10], (DEPTH, MIX_WIDTH, D_MODEL), f32) * MIX_WIDTH ** -0.5,
        "norm_ffn": 1.0 + 0.05 * nrm(ks[11], (DEPTH, D_MODEL), f32),
        "w_gate": nrm(ks[12], (DEPTH, D_MODEL, D_FF), f32) * D_MODEL ** -0.5,
        "w_up": nrm(ks[13], (DEPTH, D_MODEL, D_FF), f32) * D_MODEL ** -0.5,
        "w_down": nrm(ks[14], (DEPTH, D_FF, D_MODEL), f32) * D_FF ** -0.5,
        "norm_final": 1.0 + 0.05 * nrm(ks[15], (D_MODEL,), f32),
    }


def reference(x_prompt, x_sample, cache_k, cache_v, state_pool, rel_bias, norm_mix, w_in, w_pool,
              pool_scale, w_out, norm_ffn, w_gate, w_up, w_down, norm_final):
    xp, xs = x_prompt, x_sample
    kp_new, vp_new, pp_new, ks_new, vs_new, ps_new = [], [], [], [], [], []
    for l in range(DEPTH):
        qp, kp, vp, up = mix_input(xp, norm_mix[l], w_in[l])
        ap = dilated_attention_prompt(qp, kp, vp, rel_bias)
        pp = pool_mix(up, 0, w_pool[l], pool_scale[l])
        win = min(MAX_WINDOW, xp.shape[1])
        kp_new.append(kp[:, -win:])
        vp_new.append(vp[:, -win:])
        pp_new.append(up[:, -POOL_STATE:])
        xp = layer_tail(xp, ap, pp, w_out[l], norm_ffn[l], w_gate[l], w_up[l], w_down[l])
        qs, ksm, vsm, us = mix_input(xs, norm_mix[l], w_in[l])
        asm = dilated_attention_sample(qs, ksm, vsm, cache_k[l], cache_v[l], rel_bias)
        ucat = jnp.concatenate([state_pool[l].astype(us.dtype), us], axis=1)
        psm = pool_mix(ucat, POOL_STATE, w_pool[l], pool_scale[l])
        ks_new.append(ksm)
        vs_new.append(vsm)
        ps_new.append(ucat[:, -POOL_STATE:])
        xs = layer_tail(xs, asm, psm, w_out[l], norm_ffn[l], w_gate[l], w_up[l], w_down[l])
    y_prompt = rmsnorm(xp, norm_final)
    y_sample = rmsnorm(xs, norm_final)
    return (y_prompt, y_sample, jnp.stack(kp_new), jnp.stack(vp_new), jnp.stack(pp_new),
            jnp.stack(ks_new), jnp.stack(vs_new), jnp.stack(ps_new))
```

```python
import functools
import math

import jax
import jax.numpy as jnp
from jax import lax
from jax.experimental import pallas as pl
from jax.experimental.pallas import tpu as pltpu

HEAD_DIM = 64
N_HEADS = 16
ATTN_WIDTH = N_HEADS * HEAD_DIM
DILATED_CONFIGS = ((128, 1), (512, 4), (2048, 16))
MAX_WINDOW = 2048
BLOCK = 128
POOL_WINDOWS = (2, 4, 8, 16)
POOL_GROUP_DIM = 256
POOL_WIDTH = len(POOL_WINDOWS) * POOL_GROUP_DIM
POOL_STATE = max(POOL_WINDOWS) - 1
POOL_HIST = max(POOL_WINDOWS)
NUM_BUCKETS = 32
EPS = 1e-6
SCALE = HEAD_DIM ** -0.5

LANES = 128
HEADS_PER_TILE = LANES // HEAD_DIM
N_HEAD_TILES = N_HEADS // HEADS_PER_TILE
VMEM_LIMIT = 56 * 1024 * 1024

F32 = jnp.float32
BF16 = jnp.bfloat16


def _params(semantics):
    return pltpu.CompilerParams(dimension_semantics=semantics, vmem_limit_bytes=VMEM_LIMIT)


def _resident(shape, index_map):
    return pl.BlockSpec(shape, index_map, pipeline_mode=pl.Buffered(1))


def _rmsnorm_rows(xf, g):
    return xf * lax.rsqrt(jnp.mean(xf * xf, axis=-1, keepdims=True) + EPS) * g


def _in_proj_kernel(x_ref, g_ref, w_ref, hist_ref, wp_ref, ps_ref,
                    q_ref, k_ref, v_ref, u_ref, pool_ref,
                    h_ref, ucat_ref, pooled_ref, *, tm, tn, tiles_per_seq, shift, n_hist):
    seq_tile = pl.program_id(0) % tiles_per_seq
    hist_rows = POOL_HIST * shift

    h_ref[...] = _rmsnorm_rows(x_ref[...], g_ref[...]).astype(BF16)
    outs = (q_ref, k_ref, v_ref, u_ref)
    per_out = ATTN_WIDTH // tn
    for c in range(4 * per_out):
        res = jnp.dot(h_ref[...], w_ref[:, c * tn:(c + 1) * tn], preferred_element_type=F32)
        if c < per_out:
            res = res * SCALE
        col = (c % per_out) * tn
        outs[c // per_out][:, col:col + tn] = res

    @pl.when(seq_tile == 0)
    def _():
        ucat_ref[0:hist_rows, :] = hist_ref[...]

    ucat_ref[hist_rows:hist_rows + tm, :] = u_ref[...]

    chunk = min(128, tm)
    for r0 in range(0, tm, chunk):
        row = lax.broadcasted_iota(jnp.int32, (chunk, 1), 0)
        tok = (seq_tile * tm + r0 + row) // shift
        for g, w in enumerate(POOL_WINDOWS):
            cs = slice(g * POOL_GROUP_DIM, (g + 1) * POOL_GROUP_DIM)
            u0 = ucat_ref[hist_rows + r0:hist_rows + r0 + chunk, cs]
            acc = u0
            for j in range(1, w):
                lo = hist_rows + r0 - j * shift
                acc = acc + ucat_ref[lo:lo + chunk, cs]
            cnt = jnp.minimum(tok + (n_hist + 1), w).astype(F32)
            pooled_ref[r0:r0 + chunk, cs] = (acc / cnt - u0).astype(BF16)

    for g in range(len(POOL_WINDOWS)):
        cs = slice(g * POOL_GROUP_DIM, (g + 1) * POOL_GROUP_DIM)
        y = jnp.dot(pooled_ref[:, cs], wp_ref[g], preferred_element_type=F32) * ps_ref[:, cs]
        pool_ref[:, cs] = y.astype(BF16)

    if tiles_per_seq > 1:
        ucat_ref[0:hist_rows, :] = ucat_ref[tm:tm + hist_rows, :]


def _in_proj(x, g, w_in, hist, w_pool, pool_scale, *, tm, tiles_per_seq, shift, n_hist):
    m, d = x.shape
    in_cols = w_in.shape[1]
    hist_rows = POOL_HIST * shift
    assert m % tm == 0 and hist.shape == (hist_rows, POOL_WIDTH)
    assert tiles_per_seq == 1 or tm >= hist_rows
    row_block = lambda width: pl.BlockSpec((tm, width), lambda i: (i, 0))
    kern = functools.partial(_in_proj_kernel, tm=tm, tn=512, tiles_per_seq=tiles_per_seq,
                             shift=shift, n_hist=n_hist)
    return pl.pallas_call(
        kern,
        grid=(m // tm,),
        in_specs=[
            row_block(d),
            _resident((1, d), lambda i: (0, 0)),
            _resident((d, in_cols), lambda i: (0, 0)),
            _resident((hist_rows, POOL_WIDTH), lambda i: (0, 0)),
            _resident(w_pool.shape, lambda i: (0, 0, 0)),
            _resident((1, POOL_WIDTH), lambda i: (0, 0)),
        ],
        out_specs=[row_block(ATTN_WIDTH)] * 3 + [row_block(POOL_WIDTH)] * 2,
        out_shape=[jax.ShapeDtypeStruct((m, ATTN_WIDTH), F32)] * 3
        + [jax.ShapeDtypeStruct((m, POOL_WIDTH), F32), jax.ShapeDtypeStruct((m, POOL_WIDTH), BF16)],
        scratch_shapes=[
            pltpu.VMEM((tm, d), BF16),
            pltpu.VMEM((hist_rows + tm, POOL_WIDTH), F32),
            pltpu.VMEM((tm, POOL_WIDTH), BF16),
        ],
        compiler_params=_params(("arbitrary",)),
        name="in_proj",
    )(x, g, w_in, hist, w_pool, pool_scale)


def _attn_prompt_kernel(q_ref, k_ref, v_ref, bias_ref, o_ref, m_ref, l_ref, acc_ref, *, seq):
    is_a = lax.broadcasted_iota(jnp.int32, (BLOCK, LANES), 1) < HEAD_DIM

    def rows(start, n, dil):
        return pl.ds(start, n) if dil == 1 else pl.ds(start, n, stride=dil)

    def unit(c, dil, row0, has_prev):
        qr = rows(row0, BLOCK, dil)
        kr = rows(row0 - BLOCK * dil, 2 * BLOCK, dil) if has_prev else qr
        qb = q_ref[qr, :]
        kb = k_ref[kr, :].astype(BF16)
        vb = v_ref[kr, :].astype(BF16)
        zero = jnp.zeros_like(qb)
        ms, ss, os_ = [], [], []
        for hh in range(HEADS_PER_TILE):
            qh = jnp.where(is_a if hh == 0 else jnp.logical_not(is_a), qb, zero).astype(BF16)
            lg = lax.dot_general(qh, kb, (((1,), (1,)), ((), ())), preferred_element_type=F32)
            lg = lg + (bias_ref[c, hh] if has_prev else bias_ref[c, hh, :, BLOCK:])
            m = jnp.max(lg, axis=-1, keepdims=True)
            p = jnp.exp(lg - m)
            ms.append(m)
            ss.append(jnp.sum(p, axis=-1, keepdims=True))
            os_.append(jnp.dot(p.astype(BF16), vb, preferred_element_type=F32))
        m_new = jnp.where(is_a, ms[0], ms[1])
        s_new = jnp.where(is_a, ss[0], ss[1])
        o_new = jnp.where(is_a, os_[0], os_[1])
        if c == 0:
            m_ref[qr, :] = m_new
            l_ref[qr, :] = s_new
            acc_ref[qr, :] = o_new
        else:
            m_old = m_ref[qr, :]
            e = jnp.exp(-jnp.abs(m_old - m_new))
            keep = m_old >= m_new
            a = jnp.where(keep, 1.0, e)
            b = jnp.where(keep, e, 1.0)
            m_ref[qr, :] = jnp.maximum(m_old, m_new)
            l_ref[qr, :] = l_ref[qr, :] * a + s_new * b
            acc_ref[qr, :] = acc_ref[qr, :] * a + o_new * b

    for c, (window, dil) in enumerate(DILATED_CONFIGS):
        n_blocks = seq // dil // BLOCK
        span = BLOCK * dil

        def residue(r, c=c, dil=dil, n_blocks=n_blocks, span=span):
            unit(c, dil, r, False)
            if n_blocks > 1:
                def body(n, carry):
                    start = r + n * span
                    if dil == 1:
                        start = pl.multiple_of(start, BLOCK)
                    unit(c, dil, start, True)
                    return carry
                lax.fori_loop(1, n_blocks, body, 0)

        if n_blocks > 1:
            for r in range(dil):
                residue(r)
        else:
            def rbody(r, carry, residue=residue):
                residue(r)
                return carry
            lax.fori_loop(0, dil, rbody, 0)

    def finish(n, carry):
        r = pl.ds(pl.multiple_of(n * BLOCK, BLOCK), BLOCK)
        o_ref[r, :] = (acc_ref[r, :] / l_ref[r, :]).astype(o_ref.dtype)
        return carry
    lax.fori_loop(0, seq // BLOCK, finish, 0)


def _attn_prompt(q, k, v, bias):
    b, seq, _ = q.shape
    assert all(seq % (BLOCK * dil) == 0 for _, dil in DILATED_CONFIGS)
    blk = pl.BlockSpec((None, seq, LANES), lambda t, i: (i, 0, t))
    return pl.pallas_call(
        functools.partial(_attn_prompt_kernel, seq=seq),
        grid=(N_HEAD_TILES, b),
        in_specs=[blk, blk, blk,
                  pl.BlockSpec((len(DILATED_CONFIGS), HEADS_PER_TILE, BLOCK, 2 * BLOCK),
                               lambda t, i: (0, t, 0, 0))],
        out_specs=blk,
        out_shape=jax.ShapeDtypeStruct((b, seq, ATTN_WIDTH), BF16),
        scratch_shapes=[pltpu.VMEM((seq, LANES), F32)] * 3,
        compiler_params=_params(("arbitrary", "arbitrary")),
        name="attn_prompt",
    )(q, k, v, bias)


def _attn_sample_kernel(q_ref, kn_ref, vn_ref, ck_ref, cv_ref, bias_ref, cnt_ref, o_ref,
                        kcat_ref, vcat_ref, tail_ref, *, wc, t_new):
    for src_new, src_cache, cat in ((kn_ref, ck_ref, kcat_ref), (vn_ref, cv_ref, vcat_ref)):
        cat[0:wc, :] = src_cache[...].astype(BF16)
        tail_ref[...] = jnp.zeros_like(tail_ref)
        tail_ref[0:t_new, :] = src_new[...]
        cat[wc:wc + BLOCK, :] = tail_ref[...].astype(BF16)
    q = q_ref[...]
    is_a = lax.broadcasted_iota(jnp.int32, q.shape, 1) < HEAD_DIM
    zero = jnp.zeros_like(q)
    qbd = jnp.concatenate([jnp.where(is_a, q, zero), jnp.where(is_a, zero, q)], axis=0).astype(BF16)
    lg = lax.dot_general(qbd, kcat_ref[...], (((1,), (1,)), ((), ())), preferred_element_type=F32)
    lg = lg + bias_ref[...]
    m = jnp.max(lg, axis=-1, keepdims=True)
    p = cnt_ref[...] * jnp.exp(lg - m)
    s = jnp.sum(p, axis=-1, keepdims=True)
    o = jnp.dot(p.astype(BF16), vcat_ref[...], preferred_element_type=F32) / s
    o_ref[...] = jnp.where(is_a, o[0:t_new], o[t_new:2 * t_new])


def _attn_sample(q, kn, vn, ck, cv, bias, cnt):
    db, t_new, _ = q.shape
    wc = ck.shape[1]
    nk = wc + BLOCK
    assert HEADS_PER_TILE * t_new == 8 and wc % BLOCK == 0
    new_blk = pl.BlockSpec((None, t_new, LANES), lambda t, i: (i, 0, t))
    cache_blk = pl.BlockSpec((None, wc, LANES), lambda t, i: (i, 0, t))
    return pl.pallas_call(
        functools.partial(_attn_sample_kernel, wc=wc, t_new=t_new),
        grid=(N_HEAD_TILES, db),
        in_specs=[new_blk, new_blk, new_blk, cache_blk, cache_blk,
                  pl.BlockSpec((None, HEADS_PER_TILE * t_new, nk), lambda t, i: (t, 0, 0)),
                  pl.BlockSpec((HEADS_PER_TILE * t_new, nk), lambda t, i: (0, 0))],
        out_specs=new_blk,
        out_shape=jax.ShapeDtypeStruct((db, t_new, ATTN_WIDTH), F32),
        scratch_shapes=[pltpu.VMEM((nk, LANES), BF16), pltpu.VMEM((nk, LANES), BF16),
                        pltpu.VMEM((BLOCK, LANES), F32)],
        compiler_params=_params(("arbitrary", "arbitrary")),
        name="attn_sample",
    )(q, kn, vn, ck, cv, bias, cnt)


def _out_proj_kernel(x_ref, a_ref, p_ref, w_ref, o_ref, *, tn):
    a = a_ref[...].astype(BF16)
    p = p_ref[...]
    for c in range(o_ref.shape[1] // tn):
        cs = slice(c * tn, (c + 1) * tn)
        mixed = (jnp.dot(a, w_ref[0:ATTN_WIDTH, cs], preferred_element_type=F32)
                 + jnp.dot(p, w_ref[ATTN_WIDTH:, cs], preferred_element_type=F32))
        o_ref[:, cs] = x_ref[:, cs] + mixed


def _out_proj(x, attn, pool, w_out, *, tm):
    m, d = x.shape
    row_block = lambda width: pl.BlockSpec((tm, width), lambda i: (i, 0))
    return pl.pallas_call(
        functools.partial(_out_proj_kernel, tn=512),
        grid=(m // tm,),
        in_specs=[row_block(d), row_block(ATTN_WIDTH), row_block(POOL_WIDTH),
                  _resident(w_out.shape, lambda i: (0, 0))],
        out_specs=row_block(d),
        out_shape=jax.ShapeDtypeStruct((m, d), F32),
        compiler_params=_params(("arbitrary",)),
        name="out_proj",
    )(x, attn, pool, w_out)


def _ffn_kernel(x_ref, g_ref, wg_ref, wu_ref, wd_ref, gf_ref, o_ref, h_ref, acc_ref, *, final_norm):
    j = pl.program_id(1)

    @pl.when(j == 0)
    def _():
        xf = x_ref[...]
        h_ref[...] = _rmsnorm_rows(xf, g_ref[...]).astype(BF16)
        acc_ref[...] = xf

    h = h_ref[...]
    gate = jnp.dot(h, wg_ref[...], preferred_element_type=F32)
    up = jnp.dot(h, wu_ref[...], preferred_element_type=F32)
    act = (gate * jax.nn.sigmoid(gate) * up).astype(BF16)
    acc_ref[...] += jnp.dot(act, wd_ref[...], preferred_element_type=F32)

    @pl.when(j == pl.num_programs(1) - 1)
    def _():
        y = acc_ref[...]
        o_ref[...] = _rmsnorm_rows(y, gf_ref[...]) if final_norm else y


def _ffn(x, g, w_gate, w_up, w_down, g_final, *, tm, tf, final_norm):
    m, d = x.shape
    f = w_gate.shape[1]
    assert m % tm == 0 and f % tf == 0
    return pl.pallas_call(
        functools.partial(_ffn_kernel, final_norm=final_norm),
        grid=(m // tm, f // tf),
        in_specs=[pl.BlockSpec((tm, d), lambda i, j: (i, 0)),
                  _resident((1, d), lambda i, j: (0, 0)),
                  pl.BlockSpec((d, tf), lambda i, j: (0, j)),
                  pl.BlockSpec((d, tf), lambda i, j: (0, j)),
                  pl.BlockSpec((tf, d), lambda i, j: (j, 0)),
                  _resident((1, d), lambda i, j: (0, 0))],
        out_specs=pl.BlockSpec((tm, d), lambda i, j: (i, 0)),
        out_shape=jax.ShapeDtypeStruct((m, d), F32),
        scratch_shapes=[pltpu.VMEM((tm, d), BF16), pltpu.VMEM((tm, d), F32)],
        compiler_params=_params(("arbitrary", "arbitrary")),
        name="ffn",
    )(x, g, w_gate, w_up, w_down, g_final)


def _t5_bucket(dist):
    max_exact = NUM_BUCKETS // 2
    df = jnp.maximum(dist, 1).astype(F32)
    large = max_exact + (jnp.log(df / max_exact) / math.log(MAX_WINDOW / max_exact)
                         * (NUM_BUCKETS - max_exact)).astype(jnp.int32)
    large = jnp.minimum(large, NUM_BUCKETS - 1)
    return jnp.where(dist < max_exact, dist, large)


def _prompt_bias_tables(rel_bias):
    dist = BLOCK + jnp.arange(BLOCK)[:, None] - jnp.arange(2 * BLOCK)[None, :]
    tabs = []
    for window, dil in DILATED_CONFIGS:
        sub_w = window // dil
        assert sub_w <= BLOCK
        bias_sub = rel_bias[_t5_bucket(dil * jnp.arange(sub_w + 1))].T
        bias = bias_sub[:, jnp.clip(dist, 0, sub_w)].astype(F32)
        valid = (dist >= 0) & (dist <= sub_w)
        tabs.append(jnp.where(valid[None], bias, -jnp.inf))
    return jnp.stack(tabs)


def _sample_tables(rel_bias, wc, t_new):
    nk = wc + BLOCK
    dist = wc + jnp.arange(t_new)[:, None] - jnp.arange(nk)[None, :]
    cnt = sum(((dist >= 0) & (dist <= w) & (dist % d == 0)).astype(F32) for w, d in DILATED_CONFIGS)
    bias = rel_bias[_t5_bucket(jnp.maximum(dist, 0))].astype(F32)
    bias = jnp.where((cnt > 0)[..., None], bias, -jnp.inf)
    bias = bias.transpose(2, 0, 1).reshape(N_HEAD_TILES, HEADS_PER_TILE * t_new, nk)
    return bias, jnp.tile(cnt, (HEADS_PER_TILE, 1))


def kernel(x_prompt, x_sample, cache_k, cache_v, state_pool, rel_bias, norm_mix, w_in, w_pool,
           pool_scale, w_out, norm_ffn, w_gate, w_up, w_down, norm_final):
    b, seq, d = x_prompt.shape
    db, t_new, _ = x_sample.shape
    depth = w_in.shape[0]
    wc = cache_k.shape[2]
    assert depth >= 1 and wc == MAX_WINDOW and seq >= MAX_WINDOW and t_new <= POOL_STATE

    tm_prompt = 512
    m_sample = db * t_new
    gf = norm_final.reshape(1, d)
    prompt_bias = _prompt_bias_tables(rel_bias)
    sample_bias, sample_cnt = _sample_tables(rel_bias, wc, t_new)
    zero_hist = jnp.zeros((POOL_HIST, POOL_WIDTH), F32)

    def to_tok_major(a):
        return a.transpose(1, 0, 2).reshape(a.shape[1] * db, a.shape[2])

    def to_seq_major(a):
        return a.reshape(a.shape[0] // db, db, a.shape[1]).transpose(1, 0, 2)

    xp = x_prompt.reshape(b * seq, d)
    xs = to_tok_major(x_sample)
    outs = {name: [] for name in ("kp", "vp", "pp", "ks", "vs", "ps")}
    for l in range(depth):
        g_mix = norm_mix[l].reshape(1, d)
        g_ffn = norm_ffn[l].reshape(1, d)
        w_in_l = w_in[l].astype(BF16)
        w_pool_l = w_pool[l].astype(BF16)
        w_out_l = w_out[l].astype(BF16)
        w_gate_l, w_up_l, w_down_l = (w[l].astype(BF16) for w in (w_gate, w_up, w_down))
        scale_l = pool_scale[l].reshape(1, POOL_WIDTH)
        last = l == depth - 1

        q, k, v, u, pool = _in_proj(xp, g_mix, w_in_l, zero_hist, w_pool_l, scale_l, tm=tm_prompt,
                                    tiles_per_seq=seq // tm_prompt, shift=1, n_hist=0)
        attn = _attn_prompt(q.reshape(b, seq, -1), k.reshape(b, seq, -1), v.reshape(b, seq, -1),
                            prompt_bias)
        xp = _out_proj(xp, attn.reshape(b * seq, -1), pool, w_out_l, tm=tm_prompt)
        xp = _ffn(xp, g_ffn, w_gate_l, w_up_l, w_down_l, gf, tm=tm_prompt, tf=512, final_norm=last)
        win = min(MAX_WINDOW, seq)
        outs["kp"].append(k.reshape(b, seq, N_HEADS, HEAD_DIM)[:, -win:])
        outs["vp"].append(v.reshape(b, seq, N_HEADS, HEAD_DIM)[:, -win:])
        outs["pp"].append(u.reshape(b, seq, -1)[:, -POOL_STATE:])

        hist = jnp.concatenate([jnp.zeros((db, POOL_WIDTH), F32), to_tok_major(state_pool[l])], axis=0)
        q, k, v, u, pool = _in_proj(xs, g_mix, w_in_l, hist, w_pool_l, scale_l, tm=m_sample,
                                    tiles_per_seq=1, shift=db, n_hist=POOL_STATE)
        q, k, v, u = (to_seq_major(a) for a in (q, k, v, u))
        attn = _attn_sample(q, k, v, cache_k[l].reshape(db, wc, -1), cache_v[l].reshape(db, wc, -1),
                            sample_bias, sample_cnt)
        xs = _out_proj(xs, to_tok_major(attn), pool, w_out_l, tm=m_sample)
        xs = _ffn(xs, g_ffn, w_gate_l, w_up_l, w_down_l, gf, tm=m_sample, tf=512, final_norm=last)
        outs["ks"].append(k.reshape(db, t_new, N_HEADS, HEAD_DIM))
        outs["vs"].append(v.reshape(db, t_new, N_HEADS, HEAD_DIM))
        outs["ps"].append(jnp.concatenate([state_pool[l], u], axis=1)[:, -POOL_STATE:])

    y_prompt = xp.reshape(b, seq, d)
    y_sample = to_seq_major(xs)
    return (y_prompt, y_sample, jnp.stack(outs["kp"]), jnp.stack(outs["vp"]), jnp.stack(outs["pp"]),
            jnp.stack(outs["ks"]), jnp.stack(outs["vs"]), jnp.stack(outs["ps"]))
```

```python
import functools
import math

import numpy as np

import jax
import jax.numpy as jnp
from jax import lax
from jax.experimental import pallas as pl
from jax.experimental.pallas import tpu as pltpu

HEAD_DIM = 64
N_HEADS = 16
ATTN_WIDTH = N_HEADS * HEAD_DIM
DILATED_CONFIGS = ((128, 1), (512, 4), (2048, 16))
MAX_WINDOW = 2048
BLOCK = 128
POOL_WINDOWS = (2, 4, 8, 16)
POOL_GROUP_DIM = 256
POOL_WIDTH = len(POOL_WINDOWS) * POOL_GROUP_DIM
POOL_STATE = max(POOL_WINDOWS) - 1
POOL_HIST = max(POOL_WINDOWS)
NUM_BUCKETS = 32
EPS = 1e-6
SCALE = HEAD_DIM ** -0.5

LANES = 128
HEADS_PER_TILE = LANES // HEAD_DIM
ATTN_GROUP = 4
N_HEAD_TILES = N_HEADS // HEADS_PER_TILE
VMEM_LIMIT = 56 * 1024 * 1024

F32 = jnp.float32
BF16 = jnp.bfloat16


def _params(semantics):
    return pltpu.CompilerParams(dimension_semantics=semantics, vmem_limit_bytes=VMEM_LIMIT)


def _resident(shape, index_map):
    return pl.BlockSpec(shape, index_map, pipeline_mode=pl.Buffered(1))


def _rmsnorm_rows(xf, g):
    return xf * lax.rsqrt(jnp.mean(xf * xf, axis=-1, keepdims=True) + EPS) * g


def _in_proj_kernel(x_ref, g_ref, w_ref, hist_ref, wp_ref, ps_ref,
                    q_ref, k_ref, v_ref, u_ref, pool_ref,
                    h_ref, ucat_ref, pooled_ref, *, tm, tn, tiles_per_seq, shift, n_hist):
    seq_tile = pl.program_id(0) % tiles_per_seq
    hist_rows = POOL_HIST * shift

    h_ref[...] = _rmsnorm_rows(x_ref[...], g_ref[...]).astype(BF16)
    outs = (q_ref, k_ref, v_ref, u_ref)
    per_out = ATTN_WIDTH // tn
    for c in range(4 * per_out):
        res = jnp.dot(h_ref[...], w_ref[:, c * tn:(c + 1) * tn], preferred_element_type=F32)
        if c < per_out:
            res = res * SCALE
        col = (c % per_out) * tn
        outs[c // per_out][:, col:col + tn] = res

    @pl.when(seq_tile == 0)
    def _():
        ucat_ref[0:hist_rows, :] = hist_ref[...]

    ucat_ref[hist_rows:hist_rows + tm, :] = u_ref[...]

    chunk = min(128, tm)
    for r0 in range(0, tm, chunk):
        row = lax.broadcasted_iota(jnp.int32, (chunk, 1), 0)
        tok = (seq_tile * tm + r0 + row) // shift
        for g, w in enumerate(POOL_WINDOWS):
            cs = slice(g * POOL_GROUP_DIM, (g + 1) * POOL_GROUP_DIM)
            u0 = ucat_ref[hist_rows + r0:hist_rows + r0 + chunk, cs]
            acc = u0
            for j in range(1, w):
                lo = hist_rows + r0 - j * shift
                acc = acc + ucat_ref[lo:lo + chunk, cs]
            cnt = jnp.minimum(tok + (n_hist + 1), w).astype(F32)
            pooled_ref[r0:r0 + chunk, cs] = (acc / cnt - u0).astype(BF16)

    for g in range(len(POOL_WINDOWS)):
        cs = slice(g * POOL_GROUP_DIM, (g + 1) * POOL_GROUP_DIM)
        y = jnp.dot(pooled_ref[:, cs], wp_ref[g], preferred_element_type=F32) * ps_ref[:, cs]
        pool_ref[:, cs] = y.astype(BF16)

    if tiles_per_seq > 1:
        ucat_ref[0:hist_rows, :] = ucat_ref[tm:tm + hist_rows, :]


def _in_proj(x, g, w_in, hist, w_pool, pool_scale, *, tm, tiles_per_seq, shift, n_hist):
    m, d = x.shape
    in_cols = w_in.shape[1]
    hist_rows = POOL_HIST * shift
    assert m % tm == 0 and hist.shape == (hist_rows, POOL_WIDTH)
    assert tiles_per_seq == 1 or tm >= hist_rows
    row_block = lambda width: pl.BlockSpec((tm, width), lambda i: (i, 0))
    kern = functools.partial(_in_proj_kernel, tm=tm, tn=512, tiles_per_seq=tiles_per_seq,
                             shift=shift, n_hist=n_hist)
    return pl.pallas_call(
        kern,
        grid=(m // tm,),
        in_specs=[
            row_block(d),
            _resident((1, d), lambda i: (0, 0)),
            _resident((d, in_cols), lambda i: (0, 0)),
            _resident((hist_rows, POOL_WIDTH), lambda i: (0, 0)),
            _resident(w_pool.shape, lambda i: (0, 0, 0)),
            _resident((1, POOL_WIDTH), lambda i: (0, 0)),
        ],
        out_specs=[row_block(ATTN_WIDTH)] * 3 + [row_block(POOL_WIDTH)] * 2,
        out_shape=[jax.ShapeDtypeStruct((m, ATTN_WIDTH), F32)] * 3
        + [jax.ShapeDtypeStruct((m, POOL_WIDTH), F32), jax.ShapeDtypeStruct((m, POOL_WIDTH), BF16)],
        scratch_shapes=[
            pltpu.VMEM((tm, d), BF16),
            pltpu.VMEM((hist_rows + tm, POOL_WIDTH), F32),
            pltpu.VMEM((tm, POOL_WIDTH), BF16),
        ],
        compiler_params=_params(("arbitrary",)),
        name="in_proj",
    )(x, g, w_in, hist, w_pool, pool_scale)


def _attn_prompt_kernel(q_ref, k_ref, v_ref, bias_ref, o_ref,
                        q4_ref, k4_ref, v4_ref, m4_ref, l4_ref, acc4_ref, m_ref, l_ref, acc_ref, *, seq):
    is_a = lax.broadcasted_iota(jnp.int32, (BLOCK, LANES), 1) < HEAD_DIM
    quarter = seq // 4
    nat_cfg, r4_cfg, r16_cfg = range(3)
    assert DILATED_CONFIGS[r4_cfg][1] == 4 and DILATED_CONFIGS[r16_cfg][1] == 16

    def for_chunks(fn):
        for r in range(4):
            for j in range(quarter // BLOCK):
                fn(pl.ds(r + 4 * BLOCK * j, BLOCK, stride=4), pl.ds(r * quarter + BLOCK * j, BLOCK))

    def to_residue_major(nat, r4):
        for src, dst in zip(nat, r4):
            def move(nat_rows, r4_rows, src=src, dst=dst):
                dst[r4_rows, :] = src[nat_rows, :]
            for_chunks(move)

    def to_natural(r4, nat):
        for src, dst in zip(r4, nat):
            def move(nat_rows, r4_rows, src=src, dst=dst):
                dst[nat_rows, :] = src[r4_rows, :]
            for_chunks(move)

    def compute(c, srcs, qr, kr, has_prev):
        q_src, k_src, v_src = srcs
        qb = q_src[qr, :]
        kb = k_src[kr, :].astype(BF16)
        vb = v_src[kr, :].astype(BF16)
        zero = jnp.zeros_like(qb)
        ms, ss, os_ = [], [], []
        for hh in range(HEADS_PER_TILE):
            qh = jnp.where(is_a if hh == 0 else jnp.logical_not(is_a), qb, zero).astype(BF16)
            lg = lax.dot_general(qh, kb, (((1,), (1,)), ((), ())), preferred_element_type=F32)
            lg = lg + (bias_ref[c, hh] if has_prev else bias_ref[c, hh, :, BLOCK:])
            m = jnp.max(lg, axis=-1, keepdims=True)
            p = jnp.exp(lg - m)
            ms.append(m)
            ss.append(jnp.sum(p, axis=-1, keepdims=True))
            os_.append(jnp.dot(p.astype(BF16), vb, preferred_element_type=F32))
        return (jnp.where(is_a, ms[0], ms[1]), jnp.where(is_a, ss[0], ss[1]),
                jnp.where(is_a, os_[0], os_[1]))

    def rescale(m_old, m_new):
        e = jnp.exp(-jnp.abs(m_old - m_new))
        keep = m_old >= m_new
        return jnp.where(keep, 1.0, e), jnp.where(keep, e, 1.0)

    def first_visit(state, qr, m_new, s_new, o_new):
        for ref, val in zip(state, (m_new, s_new, o_new)):
            ref[qr, :] = val

    def merge(state, qr, m_new, s_new, o_new):
        m_st, l_st, acc_st = state
        m_old = m_st[qr, :]
        a, b = rescale(m_old, m_new)
        m_st[qr, :] = jnp.maximum(m_old, m_new)
        l_st[qr, :] = l_st[qr, :] * a + s_new * b
        acc_st[qr, :] = acc_st[qr, :] * a + o_new * b

    def last_visit(state, qr, m_new, s_new, o_new):
        m_st, l_st, acc_st = state
        a, b = rescale(m_st[qr, :], m_new)
        o_ref[qr, :] = ((acc_st[qr, :] * a + o_new * b) / (l_st[qr, :] * a + s_new * b)).astype(o_ref.dtype)

    def run(c, srcs, units, visit, state):
        for g0 in range(0, len(units), ATTN_GROUP):
            group = units[g0:g0 + ATTN_GROUP]
            done = [compute(c, srcs, *u) for u in group]
            for (qr, _, _), d in zip(group, done):
                visit(state, qr, *d)

    def block_units(base, n_blocks):
        return [(pl.ds(base + n * BLOCK, BLOCK),
                 pl.ds(base + (n - 1) * BLOCK, 2 * BLOCK) if n else pl.ds(base, BLOCK), n > 0)
                for n in range(n_blocks)]

    nat_src, r4_src = (q_ref, k_ref, v_ref), (q4_ref, k4_ref, v4_ref)
    nat_state, r4_state = (m_ref, l_ref, acc_ref), (m4_ref, l4_ref, acc4_ref)
    to_residue_major(nat_src, r4_src)

    units = [u for r in range(4) for u in block_units(r * quarter, quarter // BLOCK)]
    run(r4_cfg, r4_src, units, first_visit, r4_state)

    assert seq == 16 * BLOCK
    units = []
    for r16 in range(16):
        rows16 = pl.ds((r16 % 4) * quarter + r16 // 4, BLOCK, stride=4)
        units.append((rows16, rows16, False))
    run(r16_cfg, r4_src, units, merge, r4_state)

    to_natural(r4_state, nat_state)
    run(nat_cfg, nat_src, block_units(0, seq // BLOCK), last_visit, nat_state)


def _attn_prompt(q, k, v, bias):
    b, seq, _ = q.shape
    assert all(seq % (BLOCK * dil) == 0 for _, dil in DILATED_CONFIGS)
    blk = pl.BlockSpec((None, seq, LANES), lambda t, i: (i, 0, t))
    return pl.pallas_call(
        functools.partial(_attn_prompt_kernel, seq=seq),
        grid=(N_HEAD_TILES, b),
        in_specs=[blk, blk, blk,
                  pl.BlockSpec((len(DILATED_CONFIGS), HEADS_PER_TILE, BLOCK, 2 * BLOCK),
                               lambda t, i: (0, t, 0, 0))],
        out_specs=blk,
        out_shape=jax.ShapeDtypeStruct((b, seq, ATTN_WIDTH), BF16),
        scratch_shapes=[pltpu.VMEM((seq, LANES), F32)] * 9,
        compiler_params=_params(("arbitrary", "arbitrary")),
        name="attn_prompt",
    )(q, k, v, bias)


def _attn_sample_kernel(q_ref, kn_ref, vn_ref, ck_ref, cv_ref, bias_ref, cnt_ref, o_ref,
                        kcat_ref, vcat_ref, tail_ref, *, wc, t_new):
    for src_new, src_cache, cat in ((kn_ref, ck_ref, kcat_ref), (vn_ref, cv_ref, vcat_ref)):
        cat[0:wc, :] = src_cache[...].astype(BF16)
        tail_ref[...] = jnp.zeros_like(tail_ref)
        tail_ref[0:t_new, :] = src_new[...]
        cat[wc:wc + BLOCK, :] = tail_ref[...].astype(BF16)
    q = q_ref[...]
    is_a = lax.broadcasted_iota(jnp.int32, q.shape, 1) < HEAD_DIM
    zero = jnp.zeros_like(q)
    qbd = jnp.concatenate([jnp.where(is_a, q, zero), jnp.where(is_a, zero, q)], axis=0).astype(BF16)
    lg = lax.dot_general(qbd, kcat_ref[...], (((1,), (1,)), ((), ())), preferred_element_type=F32)
    lg = lg + bias_ref[...]
    m = jnp.max(lg, axis=-1, keepdims=True)
    p = cnt_ref[...] * jnp.exp(lg - m)
    s = jnp.sum(p, axis=-1, keepdims=True)
    o = jnp.dot(p.astype(BF16), vcat_ref[...], preferred_element_type=F32) / s
    o_ref[...] = jnp.where(is_a, o[0:t_new], o[t_new:2 * t_new])


def _attn_sample(q, kn, vn, ck, cv, bias, cnt):
    db, t_new, _ = q.shape
    wc = ck.shape[1]
    nk = wc + BLOCK
    assert HEADS_PER_TILE * t_new == 8 and wc % BLOCK == 0
    new_blk = pl.BlockSpec((None, t_new, LANES), lambda t, i: (i, 0, t))
    cache_blk = pl.BlockSpec((None, wc, LANES), lambda t, i: (i, 0, t))
    return pl.pallas_call(
        functools.partial(_attn_sample_kernel, wc=wc, t_new=t_new),
        grid=(N_HEAD_TILES, db),
        in_specs=[new_blk, new_blk, new_blk, cache_blk, cache_blk,
                  pl.BlockSpec((None, HEADS_PER_TILE * t_new, nk), lambda t, i: (t, 0, 0)),
                  pl.BlockSpec((HEADS_PER_TILE * t_new, nk), lambda t, i: (0, 0))],
        out_specs=new_blk,
        out_shape=jax.ShapeDtypeStruct((db, t_new, ATTN_WIDTH), F32),
        scratch_shapes=[pltpu.VMEM((nk, LANES), BF16), pltpu.VMEM((nk, LANES), BF16),
                        pltpu.VMEM((BLOCK, LANES), F32)],
        compiler_params=_params(("arbitrary", "arbitrary")),
        name="attn_sample",
    )(q, kn, vn, ck, cv, bias, cnt)


def _out_proj_kernel(x_ref, a_ref, p_ref, w_ref, o_ref, *, tn):
    a = a_ref[...].astype(BF16)
    p = p_ref[...]
    for c in range(o_ref.shape[1] // tn):
        cs = slice(c * tn, (c + 1) * tn)
        mixed = (jnp.dot(a, w_ref[0:ATTN_WIDTH, cs], preferred_element_type=F32)
                 + jnp.dot(p, w_ref[ATTN_WIDTH:, cs], preferred_element_type=F32))
        o_ref[:, cs] = x_ref[:, cs] + mixed


def _out_proj(x, attn, pool, w_out, *, tm):
    m, d = x.shape
    row_block = lambda width: pl.BlockSpec((tm, width), lambda i: (i, 0))
    return pl.pallas_call(
        functools.partial(_out_proj_kernel, tn=512),
        grid=(m // tm,),
        in_specs=[row_block(d), row_block(ATTN_WIDTH), row_block(POOL_WIDTH),
                  _resident(w_out.shape, lambda i: (0, 0))],
        out_specs=row_block(d),
        out_shape=jax.ShapeDtypeStruct((m, d), F32),
        compiler_params=_params(("arbitrary",)),
        name="out_proj",
    )(x, attn, pool, w_out)


def _ffn_kernel(x_ref, g_ref, wg_ref, wu_ref, wd_ref, gf_ref, o_ref, h_ref, acc_ref, *, final_norm):
    j = pl.program_id(1)

    @pl.when(j == 0)
    def _():
        xf = x_ref[...]
        h_ref[...] = _rmsnorm_rows(xf, g_ref[...]).astype(BF16)
        acc_ref[...] = xf

    h = h_ref[...]
    gate = jnp.dot(h, wg_ref[...], preferred_element_type=F32)
    up = jnp.dot(h, wu_ref[...], preferred_element_type=F32)
    act = (gate * jax.nn.sigmoid(gate) * up).astype(BF16)
    acc_ref[...] += jnp.dot(act, wd_ref[...], preferred_element_type=F32)

    @pl.when(j == pl.num_programs(1) - 1)
    def _():
        y = acc_ref[...]
        o_ref[...] = _rmsnorm_rows(y, gf_ref[...]) if final_norm else y


def _ffn(x, g, w_gate, w_up, w_down, g_final, *, tm, tf, final_norm):
    m, d = x.shape
    f = w_gate.shape[1]
    assert m % tm == 0 and f % tf == 0
    return pl.pallas_call(
        functools.partial(_ffn_kernel, final_norm=final_norm),
        grid=(m // tm, f // tf),
        in_specs=[pl.BlockSpec((tm, d), lambda i, j: (i, 0)),
                  _resident((1, d), lambda i, j: (0, 0)),
                  pl.BlockSpec((d, tf), lambda i, j: (0, j)),
                  pl.BlockSpec((d, tf), lambda i, j: (0, j)),
                  pl.BlockSpec((tf, d), lambda i, j: (j, 0)),
                  _resident((1, d), lambda i, j: (0, 0))],
        out_specs=pl.BlockSpec((tm, d), lambda i, j: (i, 0)),
        out_shape=jax.ShapeDtypeStruct((m, d), F32),
        scratch_shapes=[pltpu.VMEM((tm, d), BF16), pltpu.VMEM((tm, d), F32)],
        compiler_params=_params(("arbitrary", "arbitrary")),
        name="ffn",
    )(x, g, w_gate, w_up, w_down, g_final)


def _t5_bucket(dist):
    max_exact = NUM_BUCKETS // 2
    df = jnp.maximum(dist, 1).astype(F32)
    large = max_exact + (jnp.log(df / max_exact) / math.log(MAX_WINDOW / max_exact)
                         * (NUM_BUCKETS - max_exact)).astype(jnp.int32)
    large = jnp.minimum(large, NUM_BUCKETS - 1)
    return jnp.where(dist < max_exact, dist, large)


def _prompt_bias_tables(rel_bias):
    period = 3 * BLOCK
    tabs = []
    for window, dil in DILATED_CONFIGS:
        sub_w = window // dil
        assert sub_w <= BLOCK
        bias_sub = rel_bias[_t5_bucket(dil * jnp.arange(sub_w + 1))].T.astype(F32)
        vec = jnp.full((N_HEADS, period), -jnp.inf, F32)
        vec = vec.at[:, BLOCK - sub_w:BLOCK + 1].set(bias_sub[:, ::-1])
        flat = jnp.tile(vec, (1, BLOCK))[:, :BLOCK * (period - 1)]
        tabs.append(flat.reshape(N_HEADS, BLOCK, period - 1)[:, :, :2 * BLOCK])
    return jnp.stack(tabs)


def _sample_cache_plan(wc, t_new):
    far_dil = DILATED_CONFIGS[-1][1]
    near = DILATED_CONFIGS[-2][0]
    assert wc % far_dil == 0 and near % far_dil == 0 and t_new <= far_dil and wc > near
    far_groups = (wc - near) // far_dil
    far_pos = (np.arange(far_groups)[:, None] * far_dil + np.arange(t_new)[None, :]).reshape(-1)
    return far_groups, near, np.concatenate([far_pos, np.arange(wc - near, wc)])


def _select_cache_rows(cache, far_groups, near, t_new):
    db, wc = cache.shape[:2]
    far_dil = DILATED_CONFIGS[-1][1]
    far = cache.reshape(db, wc // far_dil, far_dil, -1)[:, :far_groups, :t_new]
    return jnp.concatenate([far.reshape(db, far_groups * t_new, -1), cache[:, wc - near:].reshape(db, near, -1)],
                           axis=1)


def _sample_tables(rel_bias, wc, t_new, cache_pos):
    tail = np.where(np.arange(BLOCK) < t_new, wc + np.arange(BLOCK), -1)
    key_pos = jnp.asarray(np.concatenate([cache_pos, tail]), jnp.int32)
    dist = wc + jnp.arange(t_new)[:, None] - key_pos[None, :]
    listed = (key_pos >= 0)[None, :] & (dist >= 0)
    cnt = sum((listed & (dist <= w) & (dist % d == 0)).astype(F32) for w, d in DILATED_CONFIGS)
    bias = rel_bias[_t5_bucket(jnp.maximum(dist, 0))].astype(F32)
    bias = jnp.where((cnt > 0)[..., None], bias, -jnp.inf)
    bias = bias.transpose(2, 0, 1).reshape(N_HEAD_TILES, HEADS_PER_TILE * t_new, key_pos.shape[0])
    return bias, jnp.tile(cnt, (HEADS_PER_TILE, 1))


def kernel(x_prompt, x_sample, cache_k, cache_v, state_pool, rel_bias, norm_mix, w_in, w_pool,
           pool_scale, w_out, norm_ffn, w_gate, w_up, w_down, norm_final):
    b, seq, d = x_prompt.shape
    db, t_new, _ = x_sample.shape
    depth = w_in.shape[0]
    wc = cache_k.shape[2]
    assert depth >= 1 and wc == MAX_WINDOW and seq >= MAX_WINDOW and t_new <= POOL_STATE

    tm_prompt = 512
    m_sample = db * t_new
    gf = norm_final.reshape(1, d)
    prompt_bias = _prompt_bias_tables(rel_bias)
    far_groups, near, cache_pos = _sample_cache_plan(wc, t_new)
    sample_bias, sample_cnt = _sample_tables(rel_bias, wc, t_new, cache_pos)
    zero_hist = jnp.zeros((POOL_HIST, POOL_WIDTH), F32)

    def to_tok_major(a):
        return a.transpose(1, 0, 2).reshape(a.shape[1] * db, a.shape[2])

    def to_seq_major(a):
        return a.reshape(a.shape[0] // db, db, a.shape[1]).transpose(1, 0, 2)

    xp = x_prompt.reshape(b * seq, d)
    xs = to_tok_major(x_sample)
    outs = {name: [] for name in ("kp", "vp", "pp", "ks", "vs", "ps")}
    for l in range(depth):
        g_mix = norm_mix[l].reshape(1, d)
        g_ffn = norm_ffn[l].reshape(1, d)
        w_in_l = w_in[l].astype(BF16)
        w_pool_l = w_pool[l].astype(BF16)
        w_out_l = w_out[l].astype(BF16)
        w_gate_l, w_up_l, w_down_l = (w[l].astype(BF16) for w in (w_gate, w_up, w_down))
        scale_l = pool_scale[l].reshape(1, POOL_WIDTH)
        last = l == depth - 1

        q, k, v, u, pool = _in_proj(xp, g_mix, w_in_l, zero_hist, w_pool_l, scale_l, tm=tm_prompt,
                                    tiles_per_seq=seq // tm_prompt, shift=1, n_hist=0)
        attn = _attn_prompt(q.reshape(b, seq, -1), k.reshape(b, seq, -1), v.reshape(b, seq, -1),
                            prompt_bias)
        xp = _out_proj(xp, attn.reshape(b * seq, -1), pool, w_out_l, tm=tm_prompt)
        xp = _ffn(xp, g_ffn, w_gate_l, w_up_l, w_down_l, gf, tm=tm_prompt, tf=512, final_norm=last)
        win = min(MAX_WINDOW, seq)
        outs["kp"].append(k.reshape(b, seq, N_HEADS, HEAD_DIM)[:, -win:])
        outs["vp"].append(v.reshape(b, seq, N_HEADS, HEAD_DIM)[:, -win:])
        outs["pp"].append(u.reshape(b, seq, -1)[:, -POOL_STATE:])

        hist = jnp.concatenate([jnp.zeros((db, POOL_WIDTH), F32), to_tok_major(state_pool[l])], axis=0)
        q, k, v, u, pool = _in_proj(xs, g_mix, w_in_l, hist, w_pool_l, scale_l, tm=m_sample,
                                    tiles_per_seq=1, shift=db, n_hist=POOL_STATE)
        q, k, v, u = (to_seq_major(a) for a in (q, k, v, u))
        attn = _attn_sample(q, k, v, _select_cache_rows(cache_k[l], far_groups, near, t_new),
                            _select_cache_rows(cache_v[l], far_groups, near, t_new),
                            sample_bias, sample_cnt)
        xs = _out_proj(xs, to_tok_major(attn), pool, w_out_l, tm=m_sample)
        xs = _ffn(xs, g_ffn, w_gate_l, w_up_l, w_down_l, gf, tm=m_sample, tf=512, final_norm=last)
        outs["ks"].append(k.reshape(db, t_new, N_HEADS, HEAD_DIM))
        outs["vs"].append(v.reshape(db, t_new, N_HEADS, HEAD_DIM))
        outs["ps"].append(jnp.concatenate([state_pool[l], u], axis=1)[:, -POOL_STATE:])

    y_prompt = xp.reshape(b, seq, d)
    y_sample = to_seq_major(xs)
    return (y_prompt, y_sample, jnp.stack(outs["kp"]), jnp.stack(outs["vp"]), jnp.stack(outs["pp"]),
            jnp.stack(outs["ks"]), jnp.stack(outs["vs"]), jnp.stack(outs["ps"]))
```

```python
import functools
import math

import numpy as np

import jax
import jax.numpy as jnp
from jax import lax
from jax.experimental import pallas as pl
from jax.experimental.pallas import tpu as pltpu

HEAD_DIM = 64
N_HEADS = 16
ATTN_WIDTH = N_HEADS * HEAD_DIM
DILATED_CONFIGS = ((128, 1), (512, 4), (2048, 16))
MAX_WINDOW = 2048
BLOCK = 128
POOL_WINDOWS = (2, 4, 8, 16)
POOL_GROUP_DIM = 256
POOL_WIDTH = len(POOL_WINDOWS) * POOL_GROUP_DIM
POOL_STATE = max(POOL_WINDOWS) - 1
POOL_HIST = max(POOL_WINDOWS)
NUM_BUCKETS = 32
EPS = 1e-6
SCALE = HEAD_DIM ** -0.5

LANES = 128
HEADS_PER_TILE = LANES // HEAD_DIM
ATTN_GROUP = 4
N_HEAD_TILES = N_HEADS // HEADS_PER_TILE
VMEM_LIMIT = 56 * 1024 * 1024

F32 = jnp.float32
BF16 = jnp.bfloat16


def _params(semantics):
    return pltpu.CompilerParams(dimension_semantics=semantics, vmem_limit_bytes=VMEM_LIMIT)


def _resident(shape, index_map):
    return pl.BlockSpec(shape, index_map, pipeline_mode=pl.Buffered(1))


def _rmsnorm_rows(xf, g):
    return xf * lax.rsqrt(jnp.mean(xf * xf, axis=-1, keepdims=True) + EPS) * g


def _in_proj_kernel(x_ref, g_ref, w_ref, hist_ref, wp_ref, ps_ref,
                    q_ref, k_ref, v_ref, u_ref, pool_ref,
                    h_ref, ucat_ref, pooled_ref, *, tm, tn, tiles_per_seq, shift, n_hist):
    seq_tile = pl.program_id(0) % tiles_per_seq
    hist_rows = POOL_HIST * shift

    h_ref[...] = _rmsnorm_rows(x_ref[...], g_ref[...]).astype(BF16)
    outs = (q_ref, k_ref, v_ref, u_ref)
    per_out = ATTN_WIDTH // tn
    for c in range(4 * per_out):
        res = jnp.dot(h_ref[...], w_ref[:, c * tn:(c + 1) * tn], preferred_element_type=F32)
        if c < per_out:
            res = res * SCALE
        col = (c % per_out) * tn
        outs[c // per_out][:, col:col + tn] = res

    @pl.when(seq_tile == 0)
    def _():
        ucat_ref[0:hist_rows, :] = hist_ref[...]

    ucat_ref[hist_rows:hist_rows + tm, :] = u_ref[...]

    chunk = min(128, tm)
    for r0 in range(0, tm, chunk):
        row = lax.broadcasted_iota(jnp.int32, (chunk, 1), 0)
        tok = (seq_tile * tm + r0 + row) // shift
        for g, w in enumerate(POOL_WINDOWS):
            cs = slice(g * POOL_GROUP_DIM, (g + 1) * POOL_GROUP_DIM)
            u0 = ucat_ref[hist_rows + r0:hist_rows + r0 + chunk, cs]
            acc = u0
            for j in range(1, w):
                lo = hist_rows + r0 - j * shift
                acc = acc + ucat_ref[lo:lo + chunk, cs]
            cnt = jnp.minimum(tok + (n_hist + 1), w).astype(F32)
            pooled_ref[r0:r0 + chunk, cs] = (acc / cnt - u0).astype(BF16)

    for g in range(len(POOL_WINDOWS)):
        cs = slice(g * POOL_GROUP_DIM, (g + 1) * POOL_GROUP_DIM)
        y = jnp.dot(pooled_ref[:, cs], wp_ref[g], preferred_element_type=F32) * ps_ref[:, cs]
        pool_ref[:, cs] = y.astype(BF16)

    if tiles_per_seq > 1:
        ucat_ref[0:hist_rows, :] = ucat_ref[tm:tm + hist_rows, :]


def _in_proj(x, g, w_in, hist, w_pool, pool_scale, *, tm, tiles_per_seq, shift, n_hist):
    m, d = x.shape
    in_cols = w_in.shape[1]
    hist_rows = POOL_HIST * shift
    assert m % tm == 0 and hist.shape == (hist_rows, POOL_WIDTH)
    assert tiles_per_seq == 1 or tm >= hist_rows
    row_block = lambda width: pl.BlockSpec((tm, width), lambda i: (i, 0))
    kern = functools.partial(_in_proj_kernel, tm=tm, tn=512, tiles_per_seq=tiles_per_seq,
                             shift=shift, n_hist=n_hist)
    return pl.pallas_call(
        kern,
        grid=(m // tm,),
        in_specs=[
            row_block(d),
            _resident((1, d), lambda i: (0, 0)),
            _resident((d, in_cols), lambda i: (0, 0)),
            _resident((hist_rows, POOL_WIDTH), lambda i: (0, 0)),
            _resident(w_pool.shape, lambda i: (0, 0, 0)),
            _resident((1, POOL_WIDTH), lambda i: (0, 0)),
        ],
        out_specs=[row_block(ATTN_WIDTH)] * 3 + [row_block(POOL_WIDTH)] * 2,
        out_shape=[jax.ShapeDtypeStruct((m, ATTN_WIDTH), F32)] * 3
        + [jax.ShapeDtypeStruct((m, POOL_WIDTH), F32), jax.ShapeDtypeStruct((m, POOL_WIDTH), BF16)],
        scratch_shapes=[
            pltpu.VMEM((tm, d), BF16),
            pltpu.VMEM((hist_rows + tm, POOL_WIDTH), F32),
            pltpu.VMEM((tm, POOL_WIDTH), BF16),
        ],
        compiler_params=_params(("arbitrary",)),
        name="in_proj",
    )(x, g, w_in, hist, w_pool, pool_scale)


def _attn_prompt_kernel(q_ref, k_ref, v_ref, bias_ref, o_ref,
                        q4_ref, k4_ref, v4_ref, m4_ref, l4_ref, acc4_ref, m_ref, l_ref, acc_ref, *, seq):
    is_a = lax.broadcasted_iota(jnp.int32, (BLOCK, LANES), 1) < HEAD_DIM
    quarter = seq // 4
    nat_cfg, r4_cfg, r16_cfg = range(3)
    assert DILATED_CONFIGS[r4_cfg][1] == 4 and DILATED_CONFIGS[r16_cfg][1] == 16

    def for_chunks(fn):
        for r in range(4):
            for j in range(quarter // BLOCK):
                fn(pl.ds(r + 4 * BLOCK * j, BLOCK, stride=4), pl.ds(r * quarter + BLOCK * j, BLOCK))

    def to_residue_major(nat, r4):
        for src, dst in zip(nat, r4):
            def move(nat_rows, r4_rows, src=src, dst=dst):
                dst[r4_rows, :] = src[nat_rows, :]
            for_chunks(move)

    def to_natural(r4, nat):
        for src, dst in zip(r4, nat):
            def move(nat_rows, r4_rows, src=src, dst=dst):
                dst[nat_rows, :] = src[r4_rows, :]
            for_chunks(move)

    def compute(c, srcs, qr, kr, has_prev):
        q_src, k_src, v_src = srcs
        qb = q_src[qr, :]
        kb = k_src[kr, :].astype(BF16)
        vb = v_src[kr, :].astype(BF16)
        zero = jnp.zeros_like(qb)
        qs = jnp.concatenate([jnp.where(is_a, qb, zero), jnp.where(is_a, zero, qb)], axis=0).astype(BF16)
        bias = bias_ref[c] if has_prev else bias_ref[c, :, :, BLOCK:]
        lg = lax.dot_general(qs, kb, (((1,), (1,)), ((), ())), preferred_element_type=F32)
        lg = lg + bias.reshape(HEADS_PER_TILE * BLOCK, bias.shape[-1])
        m = jnp.max(lg, axis=-1, keepdims=True)
        p = jnp.exp(lg - m)
        s = jnp.sum(p, axis=-1, keepdims=True)
        o = jnp.dot(p.astype(BF16), vb, preferred_element_type=F32)
        return tuple(jnp.where(is_a, x[:BLOCK], x[BLOCK:]) for x in (m, s, o))

    def rescale(m_old, m_new):
        e = jnp.exp(-jnp.abs(m_old - m_new))
        keep = m_old >= m_new
        return jnp.where(keep, 1.0, e), jnp.where(keep, e, 1.0)

    def first_visit(state, qr, m_new, s_new, o_new):
        for ref, val in zip(state, (m_new, s_new, o_new)):
            ref[qr, :] = val

    def merge(state, qr, m_new, s_new, o_new):
        m_st, l_st, acc_st = state
        m_old = m_st[qr, :]
        a, b = rescale(m_old, m_new)
        m_st[qr, :] = jnp.maximum(m_old, m_new)
        l_st[qr, :] = l_st[qr, :] * a + s_new * b
        acc_st[qr, :] = acc_st[qr, :] * a + o_new * b

    def last_visit(state, qr, m_new, s_new, o_new):
        m_st, l_st, acc_st = state
        a, b = rescale(m_st[qr, :], m_new)
        o_ref[qr, :] = ((acc_st[qr, :] * a + o_new * b) / (l_st[qr, :] * a + s_new * b)).astype(o_ref.dtype)

    def run(c, srcs, units, visit, state):
        for g0 in range(0, len(units), ATTN_GROUP):
            group = units[g0:g0 + ATTN_GROUP]
            done = [compute(c, srcs, *u) for u in group]
            for (qr, _, _), d in zip(group, done):
                visit(state, qr, *d)

    def block_units(base, n_blocks):
        return [(pl.ds(base + n * BLOCK, BLOCK),
                 pl.ds(base + (n - 1) * BLOCK, 2 * BLOCK) if n else pl.ds(base, BLOCK), n > 0)
                for n in range(n_blocks)]

    nat_src, r4_src = (q_ref, k_ref, v_ref), (q4_ref, k4_ref, v4_ref)
    nat_state, r4_state = (m_ref, l_ref, acc_ref), (m4_ref, l4_ref, acc4_ref)
    to_residue_major(nat_src, r4_src)

    units = [u for r in range(4) for u in block_units(r * quarter, quarter // BLOCK)]
    run(r4_cfg, r4_src, units, first_visit, r4_state)

    assert seq == 16 * BLOCK
    units = []
    for r16 in range(16):
        rows16 = pl.ds((r16 % 4) * quarter + r16 // 4, BLOCK, stride=4)
        units.append((rows16, rows16, False))
    run(r16_cfg, r4_src, units, merge, r4_state)

    to_natural(r4_state, nat_state)
    run(nat_cfg, nat_src, block_units(0, seq // BLOCK), last_visit, nat_state)


def _attn_prompt(q, k, v, bias):
    b, seq, _ = q.shape
    assert all(seq % (BLOCK * dil) == 0 for _, dil in DILATED_CONFIGS)
    blk = pl.BlockSpec((None, seq, LANES), lambda t, i: (i, 0, t))
    return pl.pallas_call(
        functools.partial(_attn_prompt_kernel, seq=seq),
        grid=(N_HEAD_TILES, b),
        in_specs=[blk, blk, blk,
                  pl.BlockSpec((len(DILATED_CONFIGS), HEADS_PER_TILE, BLOCK, 2 * BLOCK),
                               lambda t, i: (0, t, 0, 0))],
        out_specs=blk,
        out_shape=jax.ShapeDtypeStruct((b, seq, ATTN_WIDTH), BF16),
        scratch_shapes=[pltpu.VMEM((seq, LANES), F32)] * 9,
        compiler_params=_params(("arbitrary", "arbitrary")),
        name="attn_prompt",
    )(q, k, v, bias)


def _attn_sample_kernel(q_ref, kn_ref, vn_ref, bias_ref, cnt_ref, ck_ref, cv_ref, ck_grp_ref, cv_grp_ref,
                        o_ref, kslab, vslab, sem, *, layer, t_new, far_groups, near, wc):
    b = pl.program_id(0)
    slot = b % 2
    n_cache = far_groups * t_new + near
    far_dil = DILATED_CONFIGS[-1][1]
    assert ck_grp_ref.shape[3] == far_dil

    def gather(seq, sl):
        out = []
        for src, grp, dst, tensor in ((ck_ref, ck_grp_ref, kslab, 0), (cv_ref, cv_grp_ref, vslab, 1)):
            for h in range(N_HEADS):
                for p in range(t_new):
                    out.append(pltpu.make_async_copy(
                        grp.at[layer, seq, pl.ds(0, far_groups), p, h, :],
                        dst.at[sl, h, pl.ds(p * far_groups, far_groups), :], sem.at[sl, tensor]))
                out.append(pltpu.make_async_copy(
                    src.at[layer, seq, pl.ds(wc - near, near), h, :],
                    dst.at[sl, h, pl.ds(far_groups * t_new, near), :], sem.at[sl, tensor]))
        return out

    @pl.when(b == 0)
    def _():
        for slab in (kslab, vslab):
            slab[:, :, n_cache:, :] = jnp.zeros((2, N_HEADS, BLOCK, HEAD_DIM), F32)
        for cp in gather(0, 0):
            cp.start()

    @pl.when(b + 1 < pl.num_programs(0))
    def _():
        for cp in gather(b + 1, 1 - slot):
            cp.start()

    for cp in gather(b, slot):
        cp.wait()

    q = q_ref[...]
    first = lax.broadcasted_iota(jnp.int32, (HEADS_PER_TILE * t_new, 1), 0) < t_new
    contract_last = (((1,), (1,)), ((), ()))
    for j in range(N_HEAD_TILES):
        pair = (HEADS_PER_TILE * j, HEADS_PER_TILE * j + 1)
        cols = [slice(h * HEAD_DIM, (h + 1) * HEAD_DIM) for h in pair]
        for h, cs in zip(pair, cols):
            kslab[slot, h, n_cache:n_cache + t_new, :] = kn_ref[:, cs]
            vslab[slot, h, n_cache:n_cache + t_new, :] = vn_ref[:, cs]
        lhs = jnp.concatenate([q[:, cs] for cs in cols], axis=0).astype(BF16)
        la, lb = (lax.dot_general(lhs, kslab[slot, h].astype(BF16), contract_last,
                                  preferred_element_type=F32) for h in pair)
        lg = jnp.where(first, la, lb) + bias_ref[j]
        m = jnp.max(lg, axis=-1, keepdims=True)
        p = cnt_ref[...] * jnp.exp(lg - m)
        s = jnp.sum(p, axis=-1, keepdims=True)
        pb = p.astype(BF16)
        oa, ob = (jnp.dot(pb, vslab[slot, h].astype(BF16), preferred_element_type=F32) / s for h in pair)
        o_ref[:, cols[0]] = oa[0:t_new]
        o_ref[:, cols[1]] = ob[t_new:2 * t_new]


def _attn_sample(q, kn, vn, cache_k, cache_v, bias, cnt, *, layer, far_groups, near):
    db, t_new, width = q.shape
    depth, _, wc = cache_k.shape[:3]
    far_dil = DILATED_CONFIGS[-1][1]
    nk = far_groups * t_new + near + BLOCK
    assert HEADS_PER_TILE * t_new == 8 and bias.shape == (N_HEAD_TILES, HEADS_PER_TILE * t_new, nk)
    grouped = lambda c: c.reshape(depth, db, wc // far_dil, far_dil, N_HEADS, HEAD_DIM)
    new_blk = pl.BlockSpec((None, t_new, width), lambda i: (i, 0, 0))
    hbm = pl.BlockSpec(memory_space=pl.ANY)
    slab = pltpu.VMEM((2, N_HEADS, nk, HEAD_DIM), F32)
    return pl.pallas_call(
        functools.partial(_attn_sample_kernel, layer=layer, t_new=t_new, far_groups=far_groups,
                          near=near, wc=wc),
        grid=(db,),
        in_specs=[new_blk, new_blk, new_blk,
                  _resident(bias.shape, lambda i: (0, 0, 0)), _resident(cnt.shape, lambda i: (0, 0)),
                  hbm, hbm, hbm, hbm],
        out_specs=new_blk,
        out_shape=jax.ShapeDtypeStruct((db, t_new, width), F32),
        scratch_shapes=[slab, slab, pltpu.SemaphoreType.DMA((2, 2))],
        compiler_params=_params(("arbitrary",)),
        name="attn_sample",
    )(q, kn, vn, bias, cnt, cache_k, cache_v, grouped(cache_k), grouped(cache_v))


def _out_proj_kernel(x_ref, a_ref, p_ref, w_ref, o_ref, *, tn):
    a = a_ref[...].astype(BF16)
    p = p_ref[...]
    for c in range(o_ref.shape[1] // tn):
        cs = slice(c * tn, (c + 1) * tn)
        mixed = (jnp.dot(a, w_ref[0:ATTN_WIDTH, cs], preferred_element_type=F32)
                 + jnp.dot(p, w_ref[ATTN_WIDTH:, cs], preferred_element_type=F32))
        o_ref[:, cs] = x_ref[:, cs] + mixed


def _out_proj(x, attn, pool, w_out, *, tm):
    m, d = x.shape
    row_block = lambda width: pl.BlockSpec((tm, width), lambda i: (i, 0))
    return pl.pallas_call(
        functools.partial(_out_proj_kernel, tn=512),
        grid=(m // tm,),
        in_specs=[row_block(d), row_block(ATTN_WIDTH), row_block(POOL_WIDTH),
                  _resident(w_out.shape, lambda i: (0, 0))],
        out_specs=row_block(d),
        out_shape=jax.ShapeDtypeStruct((m, d), F32),
        compiler_params=_params(("arbitrary",)),
        name="out_proj",
    )(x, attn, pool, w_out)


def _ffn_kernel(x_ref, g_ref, wg_ref, wu_ref, wd_ref, gf_ref, o_ref, h_ref, acc_ref, *, final_norm):
    j = pl.program_id(1)

    @pl.when(j == 0)
    def _():
        xf = x_ref[...]
        h_ref[...] = _rmsnorm_rows(xf, g_ref[...]).astype(BF16)
        acc_ref[...] = xf

    h = h_ref[...]
    gate = jnp.dot(h, wg_ref[...], preferred_element_type=F32)
    up = jnp.dot(h, wu_ref[...], preferred_element_type=F32)
    act = (gate * jax.nn.sigmoid(gate) * up).astype(BF16)
    acc_ref[...] += jnp.dot(act, wd_ref[...], preferred_element_type=F32)

    @pl.when(j == pl.num_programs(1) - 1)
    def _():
        y = acc_ref[...]
        o_ref[...] = _rmsnorm_rows(y, gf_ref[...]) if final_norm else y


def _ffn(x, g, w_gate, w_up, w_down, g_final, *, tm, tf, final_norm):
    m, d = x.shape
    f = w_gate.shape[1]
    assert m % tm == 0 and f % tf == 0
    return pl.pallas_call(
        functools.partial(_ffn_kernel, final_norm=final_norm),
        grid=(m // tm, f // tf),
        in_specs=[pl.BlockSpec((tm, d), lambda i, j: (i, 0)),
                  _resident((1, d), lambda i, j: (0, 0)),
                  pl.BlockSpec((d, tf), lambda i, j: (0, j)),
                  pl.BlockSpec((d, tf), lambda i, j: (0, j)),
                  pl.BlockSpec((tf, d), lambda i, j: (j, 0)),
                  _resident((1, d), lambda i, j: (0, 0))],
        out_specs=pl.BlockSpec((tm, d), lambda i, j: (i, 0)),
        out_shape=jax.ShapeDtypeStruct((m, d), F32),
        scratch_shapes=[pltpu.VMEM((tm, d), BF16), pltpu.VMEM((tm, d), F32)],
        compiler_params=_params(("arbitrary", "arbitrary")),
        name="ffn",
    )(x, g, w_gate, w_up, w_down, g_final)


def _t5_bucket(dist):
    max_exact = NUM_BUCKETS // 2
    df = jnp.maximum(dist, 1).astype(F32)
    large = max_exact + (jnp.log(df / max_exact) / math.log(MAX_WINDOW / max_exact)
                         * (NUM_BUCKETS - max_exact)).astype(jnp.int32)
    large = jnp.minimum(large, NUM_BUCKETS - 1)
    return jnp.where(dist < max_exact, dist, large)


def _prompt_bias_tables(rel_bias):
    period = 3 * BLOCK
    tabs = []
    for window, dil in DILATED_CONFIGS:
        sub_w = window // dil
        assert sub_w <= BLOCK
        bias_sub = rel_bias[_t5_bucket(dil * jnp.arange(sub_w + 1))].T.astype(F32)
        vec = jnp.full((N_HEADS, period), -jnp.inf, F32)
        vec = vec.at[:, BLOCK - sub_w:BLOCK + 1].set(bias_sub[:, ::-1])
        flat = jnp.tile(vec, (1, BLOCK))[:, :BLOCK * (period - 1)]
        tabs.append(flat.reshape(N_HEADS, BLOCK, period - 1)[:, :, :2 * BLOCK])
    return jnp.stack(tabs)


def _sample_cache_plan(wc, t_new):
    far_dil = DILATED_CONFIGS[-1][1]
    near = DILATED_CONFIGS[-2][0]
    assert wc % far_dil == 0 and near % far_dil == 0 and t_new <= far_dil and wc > near
    far_groups = (wc - near) // far_dil
    far_pos = (np.arange(t_new)[:, None] + far_dil * np.arange(far_groups)[None, :]).reshape(-1)
    return far_groups, near, np.concatenate([far_pos, np.arange(wc - near, wc)])


def _sample_tables(rel_bias, wc, t_new, cache_pos):
    tail = np.where(np.arange(BLOCK) < t_new, wc + np.arange(BLOCK), -1)
    key_pos = jnp.asarray(np.concatenate([cache_pos, tail]), jnp.int32)
    dist = wc + jnp.arange(t_new)[:, None] - key_pos[None, :]
    listed = (key_pos >= 0)[None, :] & (dist >= 0)
    cnt = sum((listed & (dist <= w) & (dist % d == 0)).astype(F32) for w, d in DILATED_CONFIGS)
    bias = rel_bias[_t5_bucket(jnp.maximum(dist, 0))].astype(F32)
    bias = jnp.where((cnt > 0)[..., None], bias, -jnp.inf)
    bias = bias.transpose(2, 0, 1).reshape(N_HEAD_TILES, HEADS_PER_TILE * t_new, key_pos.shape[0])
    return bias, jnp.tile(cnt, (HEADS_PER_TILE, 1))


def kernel(x_prompt, x_sample, cache_k, cache_v, state_pool, rel_bias, norm_mix, w_in, w_pool,
           pool_scale, w_out, norm_ffn, w_gate, w_up, w_down, norm_final):
    b, seq, d = x_prompt.shape
    db, t_new, _ = x_sample.shape
    depth = w_in.shape[0]
    wc = cache_k.shape[2]
    assert depth >= 1 and wc == MAX_WINDOW and seq >= MAX_WINDOW and t_new <= POOL_STATE

    tm_prompt = 512
    m_sample = db * t_new
    gf = norm_final.reshape(1, d)
    prompt_bias = _prompt_bias_tables(rel_bias)
    far_groups, near, cache_pos = _sample_cache_plan(wc, t_new)
    sample_bias, sample_cnt = _sample_tables(rel_bias, wc, t_new, cache_pos)
    zero_hist = jnp.zeros((POOL_HIST, POOL_WIDTH), F32)

    def to_tok_major(a):
        return a.transpose(1, 0, 2).reshape(a.shape[1] * db, a.shape[2])

    def to_seq_major(a):
        return a.reshape(a.shape[0] // db, db, a.shape[1]).transpose(1, 0, 2)

    xp = x_prompt.reshape(b * seq, d)
    xs = to_tok_major(x_sample)
    outs = {name: [] for name in ("kp", "vp", "pp", "ks", "vs", "ps")}
    for l in range(depth):
        g_mix = norm_mix[l].reshape(1, d)
        g_ffn = norm_ffn[l].reshape(1, d)
        w_in_l = w_in[l].astype(BF16)
        w_pool_l = w_pool[l].astype(BF16)
        w_out_l = w_out[l].astype(BF16)
        w_gate_l, w_up_l, w_down_l = (w[l].astype(BF16) for w in (w_gate, w_up, w_down))
        scale_l = pool_scale[l].reshape(1, POOL_WIDTH)
        last = l == depth - 1

        q, k, v, u, pool = _in_proj(xp, g_mix, w_in_l, zero_hist, w_pool_l, scale_l, tm=tm_prompt,
                                    tiles_per_seq=seq // tm_prompt, shift=1, n_hist=0)
        attn = _attn_prompt(q.reshape(b, seq, -1), k.reshape(b, seq, -1), v.reshape(b, seq, -1),
                            prompt_bias)
        xp = _out_proj(xp, attn.reshape(b * seq, -1), pool, w_out_l, tm=tm_prompt)
        xp = _ffn(xp, g_ffn, w_gate_l, w_up_l, w_down_l, gf, tm=tm_prompt, tf=512, final_norm=last)
        win = min(MAX_WINDOW, seq)
        outs["kp"].append(k.reshape(b, seq, N_HEADS, HEAD_DIM)[:, -win:])
        outs["vp"].append(v.reshape(b, seq, N_HEADS, HEAD_DIM)[:, -win:])
        outs["pp"].append(u.reshape(b, seq, -1)[:, -POOL_STATE:])

        hist = jnp.concatenate([jnp.zeros((db, POOL_WIDTH), F32), to_tok_major(state_pool[l])], axis=0)
        q, k, v, u, pool = _in_proj(xs, g_mix, w_in_l, hist, w_pool_l, scale_l, tm=m_sample,
                                    tiles_per_seq=1, shift=db, n_hist=POOL_STATE)
        q, k, v, u = (to_seq_major(a) for a in (q, k, v, u))
        attn = _attn_sample(q, k, v, cache_k, cache_v, sample_bias, sample_cnt,
                            layer=l, far_groups=far_groups, near=near)
        xs = _out_proj(xs, to_tok_major(attn), pool, w_out_l, tm=m_sample)
        xs = _ffn(xs, g_ffn, w_gate_l, w_up_l, w_down_l, gf, tm=m_sample, tf=512, final_norm=last)
        outs["ks"].append(k.reshape(db, t_new, N_HEADS, HEAD_DIM))
        outs["vs"].append(v.reshape(db, t_new, N_HEADS, HEAD_DIM))
        outs["ps"].append(jnp.concatenate([state_pool[l], u], axis=1)[:, -POOL_STATE:])

    y_prompt = xp.reshape(b, seq, d)
    y_sample = to_seq_major(xs)
    return (y_prompt, y_sample, jnp.stack(outs["kp"]), jnp.stack(outs["vp"]), jnp.stack(outs["pp"]),
            jnp.stack(outs["ks"]), jnp.stack(outs["vs"]), jnp.stack(outs["ps"]))
```

```python
import functools
import math

import numpy as np

import jax
import jax.numpy as jnp
from jax import lax
from jax.experimental import pallas as pl
from jax.experimental.pallas import tpu as pltpu

HEAD_DIM = 64
N_HEADS = 16
ATTN_WIDTH = N_HEADS * HEAD_DIM
DILATED_CONFIGS = ((128, 1), (512, 4), (2048, 16))
MAX_WINDOW = 2048
BLOCK = 128
POOL_WINDOWS = (2, 4, 8, 16)
POOL_GROUP_DIM = 256
POOL_WIDTH = len(POOL_WINDOWS) * POOL_GROUP_DIM
POOL_STATE = max(POOL_WINDOWS) - 1
POOL_HIST = max(POOL_WINDOWS)
NUM_BUCKETS = 32
EPS = 1e-6
SCALE = HEAD_DIM ** -0.5

LANES = 128
HEADS_PER_TILE = LANES // HEAD_DIM
ATTN_GROUP = 4
N_HEAD_TILES = N_HEADS // HEADS_PER_TILE
VMEM_LIMIT = 56 * 1024 * 1024

F32 = jnp.float32
BF16 = jnp.bfloat16


def _params(semantics):
    return pltpu.CompilerParams(dimension_semantics=semantics, vmem_limit_bytes=VMEM_LIMIT)


def _resident(shape, index_map):
    return pl.BlockSpec(shape, index_map, pipeline_mode=pl.Buffered(1))


def _rmsnorm_rows(xf, g):
    return xf * lax.rsqrt(jnp.mean(xf * xf, axis=-1, keepdims=True) + EPS) * g


def _in_proj_kernel(x_ref, g_ref, w_ref, hist_ref, wp_ref, ps_ref,
                    q_ref, k_ref, v_ref, kt_ref, vt_ref, pool_ref, utail_ref,
                    h_ref, ucat_ref, pooled_ref, *, tm, tn, tiles_per_seq, shift, n_hist):
    seq_tile = pl.program_id(0) % tiles_per_seq
    hist_rows = POOL_HIST * shift
    tail_rows = utail_ref.shape[0]

    @pl.when(seq_tile == 0)
    def _():
        ucat_ref[0:hist_rows, :] = hist_ref[...]

    h_ref[...] = _rmsnorm_rows(x_ref[...], g_ref[...]).astype(BF16)
    per_out = ATTN_WIDTH // tn
    heads_per_chunk = tn // HEAD_DIM
    for c in range(4 * per_out):
        res = jnp.dot(h_ref[...], w_ref[:, c * tn:(c + 1) * tn], preferred_element_type=F32)
        which, part = divmod(c, per_out)
        cs = slice(part * tn, (part + 1) * tn)
        if which == 0:
            q_ref[:, cs] = res * SCALE
        elif which == 3:
            ucat_ref[hist_rows:hist_rows + tm, cs] = res
        else:
            nat, tr = ((k_ref, kt_ref), (v_ref, vt_ref))[which - 1]
            nat[:, cs] = res
            heads = slice(part * heads_per_chunk, (part + 1) * heads_per_chunk)
            tr[heads] = res.T.reshape(heads_per_chunk, HEAD_DIM, tm)

    utail_ref[...] = ucat_ref[hist_rows + tm - tail_rows:hist_rows + tm, :]

    chunk = min(128, tm)
    for r0 in range(0, tm, chunk):
        row = lax.broadcasted_iota(jnp.int32, (chunk, 1), 0)
        tok = (seq_tile * tm + r0 + row) // shift
        for g, w in enumerate(POOL_WINDOWS):
            cs = slice(g * POOL_GROUP_DIM, (g + 1) * POOL_GROUP_DIM)
            u0 = ucat_ref[hist_rows + r0:hist_rows + r0 + chunk, cs]
            acc = u0
            for j in range(1, w):
                lo = hist_rows + r0 - j * shift
                acc = acc + ucat_ref[lo:lo + chunk, cs]
            cnt = jnp.minimum(tok + (n_hist + 1), w).astype(F32)
            pooled_ref[r0:r0 + chunk, cs] = (acc / cnt - u0).astype(BF16)

    for g in range(len(POOL_WINDOWS)):
        cs = slice(g * POOL_GROUP_DIM, (g + 1) * POOL_GROUP_DIM)
        y = jnp.dot(pooled_ref[:, cs], wp_ref[g], preferred_element_type=F32) * ps_ref[:, cs]
        pool_ref[:, cs] = y.astype(BF16)

    if tiles_per_seq > 1:
        ucat_ref[0:hist_rows, :] = ucat_ref[tm:tm + hist_rows, :]


def _in_proj(x, g, w_in, hist, w_pool, pool_scale, *, tm, tiles_per_seq, shift, n_hist, tail_rows):
    m, d = x.shape
    in_cols = w_in.shape[1]
    hist_rows = POOL_HIST * shift
    n_tiles = m // tm
    assert m % tm == 0 and n_tiles % tiles_per_seq == 0 and hist.shape == (hist_rows, POOL_WIDTH)
    assert tiles_per_seq == 1 or tm >= hist_rows
    row_block = lambda width: pl.BlockSpec((tm, width), lambda i: (i, 0))
    tr_block = pl.BlockSpec((None, N_HEADS, HEAD_DIM, tm),
                            lambda i: (i // tiles_per_seq, 0, 0, i % tiles_per_seq))
    tr_shape = jax.ShapeDtypeStruct((n_tiles // tiles_per_seq, N_HEADS, HEAD_DIM, tiles_per_seq * tm), F32)
    kern = functools.partial(_in_proj_kernel, tm=tm, tn=512, tiles_per_seq=tiles_per_seq,
                             shift=shift, n_hist=n_hist)
    return pl.pallas_call(
        kern,
        grid=(n_tiles,),
        in_specs=[
            row_block(d),
            _resident((1, d), lambda i: (0, 0)),
            _resident((d, in_cols), lambda i: (0, 0)),
            _resident((hist_rows, POOL_WIDTH), lambda i: (0, 0)),
            _resident(w_pool.shape, lambda i: (0, 0, 0)),
            _resident((1, POOL_WIDTH), lambda i: (0, 0)),
        ],
        out_specs=[row_block(ATTN_WIDTH)] * 3 + [tr_block] * 2
        + [row_block(POOL_WIDTH), pl.BlockSpec((None, tail_rows, POOL_WIDTH), lambda i: (i, 0, 0))],
        out_shape=[jax.ShapeDtypeStruct((m, ATTN_WIDTH), F32)] * 3 + [tr_shape] * 2
        + [jax.ShapeDtypeStruct((m, POOL_WIDTH), BF16),
           jax.ShapeDtypeStruct((n_tiles, tail_rows, POOL_WIDTH), F32)],
        scratch_shapes=[
            pltpu.VMEM((tm, d), BF16),
            pltpu.VMEM((hist_rows + tm, POOL_WIDTH), F32),
            pltpu.VMEM((tm, POOL_WIDTH), BF16),
        ],
        compiler_params=_params(("arbitrary",)),
        name="in_proj",
    )(x, g, w_in, hist, w_pool, pool_scale)


def _attn_prompt_kernel(q_ref, k_ref, v_ref, bias_ref, o_ref,
                        q4_ref, k4_ref, v4_ref, m4_ref, l4_ref, acc4_ref, m_ref, l_ref, acc_ref, *, seq):
    is_a = lax.broadcasted_iota(jnp.int32, (BLOCK, LANES), 1) < HEAD_DIM
    quarter = seq // 4
    nat_cfg, r4_cfg, r16_cfg = range(3)
    assert DILATED_CONFIGS[r4_cfg][1] == 4 and DILATED_CONFIGS[r16_cfg][1] == 16

    def for_chunks(fn):
        for r in range(4):
            for j in range(quarter // BLOCK):
                fn(pl.ds(r + 4 * BLOCK * j, BLOCK, stride=4), pl.ds(r * quarter + BLOCK * j, BLOCK))

    def to_residue_major(nat, r4):
        for src, dst in zip(nat, r4):
            def move(nat_rows, r4_rows, src=src, dst=dst):
                dst[r4_rows, :] = src[nat_rows, :]
            for_chunks(move)

    def to_natural(r4, nat):
        for src, dst in zip(r4, nat):
            def move(nat_rows, r4_rows, src=src, dst=dst):
                dst[nat_rows, :] = src[r4_rows, :]
            for_chunks(move)

    def compute(c, srcs, qr, kr, has_prev):
        q_src, k_src, v_src = srcs
        qb = q_src[qr, :]
        kb = k_src[kr, :].astype(BF16)
        vb = v_src[kr, :].astype(BF16)
        zero = jnp.zeros_like(qb)
        qs = jnp.concatenate([jnp.where(is_a, qb, zero), jnp.where(is_a, zero, qb)], axis=0).astype(BF16)
        bias = bias_ref[c] if has_prev else bias_ref[c, :, :, BLOCK:]
        lg = lax.dot_general(qs, kb, (((1,), (1,)), ((), ())), preferred_element_type=F32)
        lg = lg + bias.reshape(HEADS_PER_TILE * BLOCK, bias.shape[-1])
        m = jnp.max(lg, axis=-1, keepdims=True)
        p = jnp.exp(lg - m)
        s = jnp.sum(p, axis=-1, keepdims=True)
        o = jnp.dot(p.astype(BF16), vb, preferred_element_type=F32)
        return tuple(jnp.where(is_a, x[:BLOCK], x[BLOCK:]) for x in (m, s, o))

    def rescale(m_old, m_new):
        e = jnp.exp(-jnp.abs(m_old - m_new))
        keep = m_old >= m_new
        return jnp.where(keep, 1.0, e), jnp.where(keep, e, 1.0)

    def first_visit(state, qr, m_new, s_new, o_new):
        for ref, val in zip(state, (m_new, s_new, o_new)):
            ref[qr, :] = val

    def merge(state, qr, m_new, s_new, o_new):
        m_st, l_st, acc_st = state
        m_old = m_st[qr, :]
        a, b = rescale(m_old, m_new)
        m_st[qr, :] = jnp.maximum(m_old, m_new)
        l_st[qr, :] = l_st[qr, :] * a + s_new * b
        acc_st[qr, :] = acc_st[qr, :] * a + o_new * b

    def last_visit(state, qr, m_new, s_new, o_new):
        m_st, l_st, acc_st = state
        a, b = rescale(m_st[qr, :], m_new)
        o_ref[qr, :] = ((acc_st[qr, :] * a + o_new * b) / (l_st[qr, :] * a + s_new * b)).astype(o_ref.dtype)

    def run(c, srcs, units, visit, state):
        for g0 in range(0, len(units), ATTN_GROUP):
            group = units[g0:g0 + ATTN_GROUP]
            done = [compute(c, srcs, *u) for u in group]
            for (qr, _, _), d in zip(group, done):
                visit(state, qr, *d)

    def block_units(base, n_blocks):
        return [(pl.ds(base + n * BLOCK, BLOCK),
                 pl.ds(base + (n - 1) * BLOCK, 2 * BLOCK) if n else pl.ds(base, BLOCK), n > 0)
                for n in range(n_blocks)]

    nat_src, r4_src = (q_ref, k_ref, v_ref), (q4_ref, k4_ref, v4_ref)
    nat_state, r4_state = (m_ref, l_ref, acc_ref), (m4_ref, l4_ref, acc4_ref)
    to_residue_major(nat_src, r4_src)

    units = [u for r in range(4) for u in block_units(r * quarter, quarter // BLOCK)]
    run(r4_cfg, r4_src, units, first_visit, r4_state)

    assert seq == 16 * BLOCK
    units = []
    for r16 in range(16):
        rows16 = pl.ds((r16 % 4) * quarter + r16 // 4, BLOCK, stride=4)
        units.append((rows16, rows16, False))
    run(r16_cfg, r4_src, units, merge, r4_state)

    to_natural(r4_state, nat_state)
    run(nat_cfg, nat_src, block_units(0, seq // BLOCK), last_visit, nat_state)


def _attn_prompt(q, k, v, bias):
    b, seq, _ = q.shape
    assert all(seq % (BLOCK * dil) == 0 for _, dil in DILATED_CONFIGS)
    blk = pl.BlockSpec((None, seq, LANES), lambda t, i: (i, 0, t))
    return pl.pallas_call(
        functools.partial(_attn_prompt_kernel, seq=seq),
        grid=(N_HEAD_TILES, b),
        in_specs=[blk, blk, blk,
                  pl.BlockSpec((len(DILATED_CONFIGS), HEADS_PER_TILE, BLOCK, 2 * BLOCK),
                               lambda t, i: (0, t, 0, 0))],
        out_specs=blk,
        out_shape=jax.ShapeDtypeStruct((b, seq, ATTN_WIDTH), BF16),
        scratch_shapes=[pltpu.VMEM((seq, LANES), F32)] * 9,
        compiler_params=_params(("arbitrary", "arbitrary")),
        name="attn_prompt",
    )(q, k, v, bias)


def _attn_sample_kernel(q_ref, kn_ref, vn_ref, bias_ref, cnt_ref, kt_ref, vt_ref, o_ref,
                        ktail_ref, vtail_ref, *, t_new, wc):
    @pl.when(pl.program_id(0) == 0)
    def _():
        ktail_ref[...] = jnp.zeros_like(ktail_ref)
        vtail_ref[...] = jnp.zeros_like(vtail_ref)

    ktail_ref[0:t_new, :] = kn_ref[...]
    vtail_ref[0:t_new, :] = vn_ref[...]
    q = q_ref[...]
    first = lax.broadcasted_iota(jnp.int32, (HEADS_PER_TILE * t_new, 1), 0) < t_new
    contract_last = (((1,), (1,)), ((), ()))
    cnt_cache, cnt_new = cnt_ref[:, :wc], cnt_ref[:, wc:]
    for j in range(N_HEAD_TILES):
        pair = (HEADS_PER_TILE * j, HEADS_PER_TILE * j + 1)
        cols = [slice(h * HEAD_DIM, (h + 1) * HEAD_DIM) for h in pair]
        lhs = jnp.concatenate([q[:, cs] for cs in cols], axis=0).astype(BF16)
        own = lambda a, b: jnp.where(first, a, b)
        lc = own(*(jnp.dot(lhs, kt_ref[h].astype(BF16), preferred_element_type=F32) for h in pair))
        ln = own(*(lax.dot_general(lhs, ktail_ref[:, cs].astype(BF16), contract_last,
                                   preferred_element_type=F32) for cs in cols))
        lc = lc + bias_ref[j, :, :wc]
        ln = ln + bias_ref[j, :, wc:]
        m = jnp.maximum(jnp.max(lc, axis=-1, keepdims=True), jnp.max(ln, axis=-1, keepdims=True))
        pc = cnt_cache * jnp.exp(lc - m)
        pn = cnt_new * jnp.exp(ln - m)
        s = jnp.sum(pc, axis=-1, keepdims=True) + jnp.sum(pn, axis=-1, keepdims=True)
        pc, pn = pc.astype(BF16), pn.astype(BF16)
        for h, cs, rows in zip(pair, cols, (slice(0, t_new), slice(t_new, 2 * t_new))):
            o = (lax.dot_general(pc, vt_ref[h].astype(BF16), contract_last, preferred_element_type=F32)
                 + jnp.dot(pn, vtail_ref[:, cs].astype(BF16), preferred_element_type=F32)) / s
            o_ref[:, cs] = o[rows]


def _attn_sample(q, kn, vn, cache_kt, cache_vt, bias, cnt, *, layer):
    db, t_new, width = q.shape
    wc = cache_kt.shape[-1]
    nk = wc + BLOCK
    assert HEADS_PER_TILE * t_new == 8 and bias.shape == (N_HEAD_TILES, HEADS_PER_TILE * t_new, nk)
    assert cache_kt.shape[1:] == (db, N_HEADS, HEAD_DIM, wc)
    new_blk = pl.BlockSpec((None, t_new, width), lambda i: (i, 0, 0))
    cache_blk = pl.BlockSpec((None, None, N_HEADS, HEAD_DIM, wc), lambda i: (layer, i, 0, 0, 0))
    return pl.pallas_call(
        functools.partial(_attn_sample_kernel, t_new=t_new, wc=wc),
        grid=(db,),
        in_specs=[new_blk, new_blk, new_blk,
                  _resident(bias.shape, lambda i: (0, 0, 0)), _resident(cnt.shape, lambda i: (0, 0)),
                  cache_blk, cache_blk],
        out_specs=new_blk,
        out_shape=jax.ShapeDtypeStruct((db, t_new, width), F32),
        scratch_shapes=[pltpu.VMEM((BLOCK, width), F32)] * 2,
        compiler_params=_params(("arbitrary",)),
        name="attn_sample",
    )(q, kn, vn, bias, cnt, cache_kt, cache_vt)


def _out_proj_kernel(x_ref, a_ref, p_ref, w_ref, o_ref, *, tn):
    a = a_ref[...].astype(BF16)
    p = p_ref[...]
    for c in range(o_ref.shape[1] // tn):
        cs = slice(c * tn, (c + 1) * tn)
        mixed = (jnp.dot(a, w_ref[0:ATTN_WIDTH, cs], preferred_element_type=F32)
                 + jnp.dot(p, w_ref[ATTN_WIDTH:, cs], preferred_element_type=F32))
        o_ref[:, cs] = x_ref[:, cs] + mixed


def _out_proj(x, attn, pool, w_out, *, tm):
    m, d = x.shape
    row_block = lambda width: pl.BlockSpec((tm, width), lambda i: (i, 0))
    return pl.pallas_call(
        functools.partial(_out_proj_kernel, tn=512),
        grid=(m // tm,),
        in_specs=[row_block(d), row_block(ATTN_WIDTH), row_block(POOL_WIDTH),
                  _resident(w_out.shape, lambda i: (0, 0))],
        out_specs=row_block(d),
        out_shape=jax.ShapeDtypeStruct((m, d), F32),
        compiler_params=_params(("arbitrary",)),
        name="out_proj",
    )(x, attn, pool, w_out)


def _ffn_kernel(x_ref, g_ref, wg_ref, wu_ref, wd_ref, gf_ref, o_ref, h_ref, acc_ref, *, final_norm):
    j = pl.program_id(1)

    @pl.when(j == 0)
    def _():
        xf = x_ref[...]
        h_ref[...] = _rmsnorm_rows(xf, g_ref[...]).astype(BF16)
        acc_ref[...] = xf

    h = h_ref[...]
    gate = jnp.dot(h, wg_ref[...], preferred_element_type=F32)
    up = jnp.dot(h, wu_ref[...], preferred_element_type=F32)
    act = (gate * jax.nn.sigmoid(gate) * up).astype(BF16)
    acc_ref[...] += jnp.dot(act, wd_ref[...], preferred_element_type=F32)

    @pl.when(j == pl.num_programs(1) - 1)
    def _():
        y = acc_ref[...]
        o_ref[...] = _rmsnorm_rows(y, gf_ref[...]) if final_norm else y


def _ffn(x, g, w_gate, w_up, w_down, g_final, *, tm, tf, final_norm):
    m, d = x.shape
    f = w_gate.shape[1]
    assert m % tm == 0 and f % tf == 0
    return pl.pallas_call(
        functools.partial(_ffn_kernel, final_norm=final_norm),
        grid=(m // tm, f // tf),
        in_specs=[pl.BlockSpec((tm, d), lambda i, j: (i, 0)),
                  _resident((1, d), lambda i, j: (0, 0)),
                  pl.BlockSpec((d, tf), lambda i, j: (0, j)),
                  pl.BlockSpec((d, tf), lambda i, j: (0, j)),
                  pl.BlockSpec((tf, d), lambda i, j: (j, 0)),
                  _resident((1, d), lambda i, j: (0, 0))],
        out_specs=pl.BlockSpec((tm, d), lambda i, j: (i, 0)),
        out_shape=jax.ShapeDtypeStruct((m, d), F32),
        scratch_shapes=[pltpu.VMEM((tm, d), BF16), pltpu.VMEM((tm, d), F32)],
        compiler_params=_params(("arbitrary", "arbitrary")),
        name="ffn",
    )(x, g, w_gate, w_up, w_down, g_final)


def _t5_bucket(dist):
    max_exact = NUM_BUCKETS // 2
    df = jnp.maximum(dist, 1).astype(F32)
    large = max_exact + (jnp.log(df / max_exact) / math.log(MAX_WINDOW / max_exact)
                         * (NUM_BUCKETS - max_exact)).astype(jnp.int32)
    large = jnp.minimum(large, NUM_BUCKETS - 1)
    return jnp.where(dist < max_exact, dist, large)


def _prompt_bias_tables(rel_bias):
    period = 3 * BLOCK
    tabs = []
    for window, dil in DILATED_CONFIGS:
        sub_w = window // dil
        assert sub_w <= BLOCK
        bias_sub = rel_bias[_t5_bucket(dil * jnp.arange(sub_w + 1))].T.astype(F32)
        vec = jnp.full((N_HEADS, period), -jnp.inf, F32)
        vec = vec.at[:, BLOCK - sub_w:BLOCK + 1].set(bias_sub[:, ::-1])
        flat = jnp.tile(vec, (1, BLOCK))[:, :BLOCK * (period - 1)]
        tabs.append(flat.reshape(N_HEADS, BLOCK, period - 1)[:, :, :2 * BLOCK])
    return jnp.stack(tabs)


def _sample_tables(rel_bias, wc, t_new):
    tail = np.where(np.arange(BLOCK) < t_new, wc + np.arange(BLOCK), -1)
    key_pos = jnp.asarray(np.concatenate([np.arange(wc), tail]), jnp.int32)
    dist = wc + jnp.arange(t_new)[:, None] - key_pos[None, :]
    listed = (key_pos >= 0)[None, :] & (dist >= 0)
    cnt = sum((listed & (dist <= w) & (dist % d == 0)).astype(F32) for w, d in DILATED_CONFIGS)
    bias = rel_bias[_t5_bucket(jnp.maximum(dist, 0))].astype(F32)
    bias = jnp.where((cnt > 0)[..., None], bias, -jnp.inf)
    bias = bias.transpose(2, 0, 1).reshape(N_HEAD_TILES, HEADS_PER_TILE * t_new, key_pos.shape[0])
    return bias, jnp.tile(cnt, (HEADS_PER_TILE, 1))


def kernel(x_prompt, x_sample, cache_k, cache_v, state_pool, rel_bias, norm_mix, w_in, w_pool,
           pool_scale, w_out, norm_ffn, w_gate, w_up, w_down, norm_final):
    b, seq, d = x_prompt.shape
    db, t_new, _ = x_sample.shape
    depth = w_in.shape[0]
    wc = cache_k.shape[2]
    assert depth >= 1 and wc == MAX_WINDOW and seq >= MAX_WINDOW and t_new <= POOL_STATE

    tm_prompt = 512
    m_sample = db * t_new
    gf = norm_final.reshape(1, d)
    prompt_bias = _prompt_bias_tables(rel_bias)
    sample_bias, sample_cnt = _sample_tables(rel_bias, wc, t_new)
    cache_kt, cache_vt = (c.transpose(0, 1, 3, 4, 2) for c in (cache_k, cache_v))
    zero_hist = jnp.zeros((POOL_HIST, POOL_WIDTH), F32)

    def to_tok_major(a):
        return a.transpose(1, 0, 2).reshape(a.shape[1] * db, a.shape[2])

    def to_seq_major(a):
        return a.reshape(a.shape[0] // db, db, a.shape[1]).transpose(1, 0, 2)

    xp = x_prompt.reshape(b * seq, d)
    xs = to_tok_major(x_sample)
    outs = {name: [] for name in ("kp", "vp", "pp", "ks", "vs", "ps")}
    for l in range(depth):
        g_mix = norm_mix[l].reshape(1, d)
        g_ffn = norm_ffn[l].reshape(1, d)
        w_in_l = w_in[l].astype(BF16)
        w_pool_l = w_pool[l].astype(BF16)
        w_out_l = w_out[l].astype(BF16)
        w_gate_l, w_up_l, w_down_l = (w[l].astype(BF16) for w in (w_gate, w_up, w_down))
        scale_l = pool_scale[l].reshape(1, POOL_WIDTH)
        last = l == depth - 1

        tiles_per_seq = seq // tm_prompt
        q, k, v, kt, vt, pool, utail = _in_proj(
            xp, g_mix, w_in_l, zero_hist, w_pool_l, scale_l, tm=tm_prompt, tiles_per_seq=tiles_per_seq,
            shift=1, n_hist=0, tail_rows=POOL_HIST)
        attn = _attn_prompt(q.reshape(b, seq, -1), k.reshape(b, seq, -1), v.reshape(b, seq, -1),
                            prompt_bias)
        xp = _out_proj(xp, attn.reshape(b * seq, -1), pool, w_out_l, tm=tm_prompt)
        xp = _ffn(xp, g_ffn, w_gate_l, w_up_l, w_down_l, gf, tm=tm_prompt, tf=512, final_norm=last)
        win = min(MAX_WINDOW, seq)
        outs["kp"].append(kt.transpose(0, 3, 1, 2)[:, -win:])
        outs["vp"].append(vt.transpose(0, 3, 1, 2)[:, -win:])
        outs["pp"].append(utail.reshape(b, tiles_per_seq, POOL_HIST, -1)[:, -1, -POOL_STATE:])

        hist = jnp.concatenate([jnp.zeros((db, POOL_WIDTH), F32), to_tok_major(state_pool[l])], axis=0)
        q, k, v, _, _, pool, utail = _in_proj(
            xs, g_mix, w_in_l, hist, w_pool_l, scale_l, tm=m_sample, tiles_per_seq=1, shift=db,
            n_hist=POOL_STATE, tail_rows=m_sample)
        q, k, v, u = (to_seq_major(a) for a in (q, k, v, utail[0]))
        attn = _attn_sample(q, k, v, cache_kt, cache_vt, sample_bias, sample_cnt, layer=l)
        xs = _out_proj(xs, to_tok_major(attn), pool, w_out_l, tm=m_sample)
        xs = _ffn(xs, g_ffn, w_gate_l, w_up_l, w_down_l, gf, tm=m_sample, tf=512, final_norm=last)
        outs["ks"].append(k.reshape(db, t_new, N_HEADS, HEAD_DIM))
        outs["vs"].append(v.reshape(db, t_new, N_HEADS, HEAD_DIM))
        outs["ps"].append(jnp.concatenate([state_pool[l], u], axis=1)[:, -POOL_STATE:])

    y_prompt = xp.reshape(b, seq, d)
    y_sample = to_seq_major(xs)
    return (y_prompt, y_sample, jnp.stack(outs["kp"]), jnp.stack(outs["vp"]), jnp.stack(outs["pp"]),
            jnp.stack(outs["ks"]), jnp.stack(outs["vs"]), jnp.stack(outs["ps"]))
```

```python
import functools
import math

import numpy as np

import jax
import jax.numpy as jnp
from jax import lax
from jax.experimental import pallas as pl
from jax.experimental.pallas import tpu as pltpu

HEAD_DIM = 64
N_HEADS = 16
ATTN_WIDTH = N_HEADS * HEAD_DIM
DILATED_CONFIGS = ((128, 1), (512, 4), (2048, 16))
MAX_WINDOW = 2048
BLOCK = 128
POOL_WINDOWS = (2, 4, 8, 16)
POOL_GROUP_DIM = 256
POOL_WIDTH = len(POOL_WINDOWS) * POOL_GROUP_DIM
POOL_STATE = max(POOL_WINDOWS) - 1
POOL_HIST = max(POOL_WINDOWS)
NUM_BUCKETS = 32
EPS = 1e-6
SCALE = HEAD_DIM ** -0.5
LOG2E = math.log2(math.e)

LANES = 128
HEADS_PER_TILE = LANES // HEAD_DIM
ATTN_GROUP = 4
N_HEAD_TILES = N_HEADS // HEADS_PER_TILE
VMEM_LIMIT = 56 * 1024 * 1024

F32 = jnp.float32
BF16 = jnp.bfloat16


def _params(semantics):
    return pltpu.CompilerParams(dimension_semantics=semantics, vmem_limit_bytes=VMEM_LIMIT)


def _resident(shape, index_map):
    return pl.BlockSpec(shape, index_map, pipeline_mode=pl.Buffered(1))


def _rmsnorm_rows(xf, g):
    return xf * lax.rsqrt(jnp.mean(xf * xf, axis=-1, keepdims=True) + EPS) * g


def _in_proj_kernel(x_ref, g_ref, w_ref, hist_ref, wp_ref, ps_ref,
                    q_ref, k_ref, v_ref, kt_ref, vt_ref, pool_ref, utail_ref,
                    h_ref, ucat_ref, pooled_ref, *, tm, tn, tiles_per_seq, shift, n_hist):
    seq_tile = pl.program_id(0) % tiles_per_seq
    hist_rows = POOL_HIST * shift
    tail_rows = utail_ref.shape[0]

    @pl.when(seq_tile == 0)
    def _():
        ucat_ref[0:hist_rows, :] = hist_ref[...]

    h_ref[...] = _rmsnorm_rows(x_ref[...], g_ref[...]).astype(BF16)
    per_out = ATTN_WIDTH // tn
    heads_per_chunk = tn // HEAD_DIM
    for c in list(range(3 * per_out, 4 * per_out)) + list(range(3 * per_out)):
        res = jnp.dot(h_ref[...], w_ref[:, c * tn:(c + 1) * tn], preferred_element_type=F32)
        which, part = divmod(c, per_out)
        cs = slice(part * tn, (part + 1) * tn)
        if which == 0:
            q_ref[:, cs] = res * (SCALE * LOG2E)
        elif which == 3:
            ucat_ref[hist_rows:hist_rows + tm, cs] = res
        else:
            nat, tr = ((k_ref, kt_ref), (v_ref, vt_ref))[which - 1]
            nat[:, cs] = res
            heads = slice(part * heads_per_chunk, (part + 1) * heads_per_chunk)
            tr[heads] = res.T.reshape(heads_per_chunk, HEAD_DIM, tm)

    utail_ref[...] = ucat_ref[hist_rows + tm - tail_rows:hist_rows + tm, :]

    chunk = min(128, tm)
    for r0 in range(0, tm, chunk):
        row = lax.broadcasted_iota(jnp.int32, (chunk, 1), 0)
        tok = (seq_tile * tm + r0 + row) // shift
        for g, w in enumerate(POOL_WINDOWS):
            cs = slice(g * POOL_GROUP_DIM, (g + 1) * POOL_GROUP_DIM)
            u0 = ucat_ref[hist_rows + r0:hist_rows + r0 + chunk, cs]
            acc = u0
            for j in range(1, w):
                lo = hist_rows + r0 - j * shift
                acc = acc + ucat_ref[lo:lo + chunk, cs]
            cnt = jnp.minimum(tok + (n_hist + 1), w).astype(F32)
            pooled_ref[r0:r0 + chunk, cs] = (acc / cnt - u0).astype(BF16)

    for g in range(len(POOL_WINDOWS)):
        cs = slice(g * POOL_GROUP_DIM, (g + 1) * POOL_GROUP_DIM)
        y = jnp.dot(pooled_ref[:, cs], wp_ref[g], preferred_element_type=F32) * ps_ref[:, cs]
        pool_ref[:, cs] = y.astype(BF16)

    if tiles_per_seq > 1:
        ucat_ref[0:hist_rows, :] = ucat_ref[tm:tm + hist_rows, :]


def _in_proj(x, g, w_in, hist, w_pool, pool_scale, *, tm, tiles_per_seq, shift, n_hist, tail_rows):
    m, d = x.shape
    in_cols = w_in.shape[1]
    hist_rows = POOL_HIST * shift
    n_tiles = m // tm
    assert m % tm == 0 and n_tiles % tiles_per_seq == 0 and hist.shape == (hist_rows, POOL_WIDTH)
    assert tiles_per_seq == 1 or tm >= hist_rows
    row_block = lambda width: pl.BlockSpec((tm, width), lambda i: (i, 0))
    tr_block = pl.BlockSpec((None, N_HEADS, HEAD_DIM, tm),
                            lambda i: (i // tiles_per_seq, 0, 0, i % tiles_per_seq))
    tr_shape = jax.ShapeDtypeStruct((n_tiles // tiles_per_seq, N_HEADS, HEAD_DIM, tiles_per_seq * tm), F32)
    kern = functools.partial(_in_proj_kernel, tm=tm, tn=512, tiles_per_seq=tiles_per_seq,
                             shift=shift, n_hist=n_hist)
    return pl.pallas_call(
        kern,
        grid=(n_tiles,),
        in_specs=[
            row_block(d),
            _resident((1, d), lambda i: (0, 0)),
            _resident((d, in_cols), lambda i: (0, 0)),
            _resident((hist_rows, POOL_WIDTH), lambda i: (0, 0)),
            _resident(w_pool.shape, lambda i: (0, 0, 0)),
            _resident((1, POOL_WIDTH), lambda i: (0, 0)),
        ],
        out_specs=[row_block(ATTN_WIDTH)] * 3 + [tr_block] * 2
        + [row_block(POOL_WIDTH), pl.BlockSpec((None, tail_rows, POOL_WIDTH), lambda i: (i, 0, 0))],
        out_shape=[jax.ShapeDtypeStruct((m, ATTN_WIDTH), F32)] * 3 + [tr_shape] * 2
        + [jax.ShapeDtypeStruct((m, POOL_WIDTH), BF16),
           jax.ShapeDtypeStruct((n_tiles, tail_rows, POOL_WIDTH), F32)],
        scratch_shapes=[
            pltpu.VMEM((tm, d), BF16),
            pltpu.VMEM((hist_rows + tm, POOL_WIDTH), F32),
            pltpu.VMEM((tm, POOL_WIDTH), BF16),
        ],
        compiler_params=_params(("arbitrary",)),
        name="in_proj",
    )(x, g, w_in, hist, w_pool, pool_scale)


def _attn_prompt_kernel(q_ref, k_ref, v_ref, bias_ref, o_ref,
                        q4_ref, k4_ref, v4_ref, m4_ref, l4_ref, acc4_ref, m_ref, l_ref, acc_ref, *, seq):
    is_a = lax.broadcasted_iota(jnp.int32, (BLOCK, LANES), 1) < HEAD_DIM
    quarter = seq // 4
    nat_cfg, r4_cfg, r16_cfg = range(3)
    assert DILATED_CONFIGS[r4_cfg][1] == 4 and DILATED_CONFIGS[r16_cfg][1] == 16

    def for_chunks(fn):
        for r in range(4):
            for j in range(quarter // BLOCK):
                fn(pl.ds(r + 4 * BLOCK * j, BLOCK, stride=4), pl.ds(r * quarter + BLOCK * j, BLOCK))

    def to_residue_major(nat, r4):
        for src, dst in zip(nat, r4):
            def move(nat_rows, r4_rows, src=src, dst=dst):
                dst[r4_rows, :] = src[nat_rows, :]
            for_chunks(move)

    def to_natural(r4, nat):
        for src, dst in zip(r4, nat):
            def move(nat_rows, r4_rows, src=src, dst=dst):
                dst[nat_rows, :] = src[r4_rows, :]
            for_chunks(move)

    def compute(c, srcs, qr, kr, has_prev):
        q_src, k_src, v_src = srcs
        qb = q_src[qr, :]
        kb = k_src[kr, :].astype(BF16)
        vb = v_src[kr, :].astype(BF16)
        vb = jnp.concatenate([vb, jnp.ones_like(vb)], axis=1)
        zero = jnp.zeros_like(qb)
        qs = jnp.concatenate([jnp.where(is_a, qb, zero), jnp.where(is_a, zero, qb)], axis=0).astype(BF16)
        bias = bias_ref[c] if has_prev else bias_ref[c, :, :, BLOCK:]
        lg = lax.dot_general(qs, kb, (((1,), (1,)), ((), ())), preferred_element_type=F32)
        lg = lg + bias.reshape(HEADS_PER_TILE * BLOCK, bias.shape[-1])
        m = jnp.max(lg, axis=-1, keepdims=True)
        p = jnp.exp2(lg - m)
        o = jnp.dot(p.astype(BF16), vb, preferred_element_type=F32)
        return tuple(jnp.where(is_a, x[:BLOCK], x[BLOCK:]) for x in (m, o[:, LANES:], o[:, :LANES]))

    def rescale(m_old, m_new):
        e = jnp.exp2(-jnp.abs(m_old - m_new))
        keep = m_old >= m_new
        return jnp.where(keep, 1.0, e), jnp.where(keep, e, 1.0)

    def first_visit(state, qr, m_new, s_new, o_new):
        for ref, val in zip(state, (m_new, s_new, o_new)):
            ref[qr, :] = val

    def merge(state, qr, m_new, s_new, o_new):
        m_st, l_st, acc_st = state
        m_old = m_st[qr, :]
        a, b = rescale(m_old, m_new)
        m_st[qr, :] = jnp.maximum(m_old, m_new)
        l_st[qr, :] = l_st[qr, :] * a + s_new * b
        acc_st[qr, :] = acc_st[qr, :] * a + o_new * b

    def last_visit(state, qr, m_new, s_new, o_new):
        m_st, l_st, acc_st = state
        a, b = rescale(m_st[qr, :], m_new)
        o_ref[qr, :] = ((acc_st[qr, :] * a + o_new * b) / (l_st[qr, :] * a + s_new * b)).astype(o_ref.dtype)

    def run(c, srcs, units, visit, state):
        for g0 in range(0, len(units), ATTN_GROUP):
            group = units[g0:g0 + ATTN_GROUP]
            done = [compute(c, srcs, *u) for u in group]
            for (qr, _, _), d in zip(group, done):
                visit(state, qr, *d)

    def block_units(base, n_blocks):
        return [(pl.ds(base + n * BLOCK, BLOCK),
                 pl.ds(base + (n - 1) * BLOCK, 2 * BLOCK) if n else pl.ds(base, BLOCK), n > 0)
                for n in range(n_blocks)]

    nat_src, r4_src = (q_ref, k_ref, v_ref), (q4_ref, k4_ref, v4_ref)
    nat_state, r4_state = (m_ref, l_ref, acc_ref), (m4_ref, l4_ref, acc4_ref)
    to_residue_major(nat_src, r4_src)

    units = [u for r in range(4) for u in block_units(r * quarter, quarter // BLOCK)]
    run(r4_cfg, r4_src, units, first_visit, r4_state)

    assert seq == 16 * BLOCK
    units = []
    for r16 in range(16):
        rows16 = pl.ds((r16 % 4) * quarter + r16 // 4, BLOCK, stride=4)
        units.append((rows16, rows16, False))
    run(r16_cfg, r4_src, units, merge, r4_state)

    to_natural(r4_state, nat_state)
    run(nat_cfg, nat_src, block_units(0, seq // BLOCK), last_visit, nat_state)


def _attn_prompt(q, k, v, bias):
    b, seq, _ = q.shape
    assert all(seq % (BLOCK * dil) == 0 for _, dil in DILATED_CONFIGS)
    blk = pl.BlockSpec((None, seq, LANES), lambda t, i: (i, 0, t))
    return pl.pallas_call(
        functools.partial(_attn_prompt_kernel, seq=seq),
        grid=(N_HEAD_TILES, b),
        in_specs=[blk, blk, blk,
                  pl.BlockSpec((len(DILATED_CONFIGS), HEADS_PER_TILE, BLOCK, 2 * BLOCK),
                               lambda t, i: (0, t, 0, 0))],
        out_specs=blk,
        out_shape=jax.ShapeDtypeStruct((b, seq, ATTN_WIDTH), BF16),
        scratch_shapes=[pltpu.VMEM((seq, LANES), F32)] * 9,
        compiler_params=_params(("arbitrary", "arbitrary")),
        name="attn_prompt",
    )(q, k, v, bias)


def _attn_sample_kernel(q_ref, kn_ref, vn_ref, bias_ref, cnt_ref, kt_ref, vt_ref, o_ref,
                        ktail_ref, vtail_ref, *, t_new, wc):
    @pl.when(pl.program_id(0) == 0)
    def _():
        ktail_ref[...] = jnp.zeros_like(ktail_ref)
        vtail_ref[...] = jnp.zeros_like(vtail_ref)

    ktail_ref[0:t_new, :] = kn_ref[...]
    vtail_ref[0:t_new, :] = vn_ref[...]
    q = q_ref[...]
    first = lax.broadcasted_iota(jnp.int32, (HEADS_PER_TILE * t_new, 1), 0) < t_new
    contract_last = (((1,), (1,)), ((), ()))
    cnt_cache, cnt_new = cnt_ref[:, :wc], cnt_ref[:, wc:]
    for j in range(N_HEAD_TILES):
        pair = (HEADS_PER_TILE * j, HEADS_PER_TILE * j + 1)
        cols = [slice(h * HEAD_DIM, (h + 1) * HEAD_DIM) for h in pair]
        lhs = jnp.concatenate([q[:, cs] for cs in cols], axis=0).astype(BF16)
        own = lambda a, b: jnp.where(first, a, b)
        lc = own(*(jnp.dot(lhs, kt_ref[h].astype(BF16), preferred_element_type=F32) for h in pair))
        ln = own(*(lax.dot_general(lhs, ktail_ref[:, cs].astype(BF16), contract_last,
                                   preferred_element_type=F32) for cs in cols))
        lc = lc + bias_ref[j, :, :wc]
        ln = ln + bias_ref[j, :, wc:]
        m = jnp.maximum(jnp.max(lc, axis=-1, keepdims=True), jnp.max(ln, axis=-1, keepdims=True))
        pc = cnt_cache * jnp.exp2(lc - m)
        pn = cnt_new * jnp.exp2(ln - m)
        s = jnp.sum(pc, axis=-1, keepdims=True) + jnp.sum(pn, axis=-1, keepdims=True)
        pc, pn = pc.astype(BF16), pn.astype(BF16)
        for h, cs, rows in zip(pair, cols, (slice(0, t_new), slice(t_new, 2 * t_new))):
            o = (lax.dot_general(pc, vt_ref[h].astype(BF16), contract_last, preferred_element_type=F32)
                 + jnp.dot(pn, vtail_ref[:, cs].astype(BF16), preferred_element_type=F32)) / s
            o_ref[:, cs] = o[rows]


def _attn_sample(q, kn, vn, cache_kt, cache_vt, bias, cnt, *, layer):
    db, t_new, width = q.shape
    wc = cache_kt.shape[-1]
    nk = wc + BLOCK
    assert HEADS_PER_TILE * t_new == 8 and bias.shape == (N_HEAD_TILES, HEADS_PER_TILE * t_new, nk)
    assert cache_kt.shape[1:] == (db, N_HEADS, HEAD_DIM, wc)
    new_blk = pl.BlockSpec((None, t_new, width), lambda i: (i, 0, 0))
    cache_blk = pl.BlockSpec((None, None, N_HEADS, HEAD_DIM, wc), lambda i: (layer, i, 0, 0, 0))
    return pl.pallas_call(
        functools.partial(_attn_sample_kernel, t_new=t_new, wc=wc),
        grid=(db,),
        in_specs=[new_blk, new_blk, new_blk,
                  _resident(bias.shape, lambda i: (0, 0, 0)), _resident(cnt.shape, lambda i: (0, 0)),
                  cache_blk, cache_blk],
        out_specs=new_blk,
        out_shape=jax.ShapeDtypeStruct((db, t_new, width), F32),
        scratch_shapes=[pltpu.VMEM((BLOCK, width), F32)] * 2,
        compiler_params=_params(("arbitrary",)),
        name="attn_sample",
    )(q, kn, vn, bias, cnt, cache_kt, cache_vt)


def _out_proj_kernel(x_ref, a_ref, p_ref, w_ref, o_ref, *, tn):
    a = a_ref[...].astype(BF16)
    p = p_ref[...]
    for c in range(o_ref.shape[1] // tn):
        cs = slice(c * tn, (c + 1) * tn)
        mixed = (jnp.dot(a, w_ref[0:ATTN_WIDTH, cs], preferred_element_type=F32)
                 + jnp.dot(p, w_ref[ATTN_WIDTH:, cs], preferred_element_type=F32))
        o_ref[:, cs] = x_ref[:, cs] + mixed


def _out_proj(x, attn, pool, w_out, *, tm):
    m, d = x.shape
    row_block = lambda width: pl.BlockSpec((tm, width), lambda i: (i, 0))
    return pl.pallas_call(
        functools.partial(_out_proj_kernel, tn=512),
        grid=(m // tm,),
        in_specs=[row_block(d), row_block(ATTN_WIDTH), row_block(POOL_WIDTH),
                  _resident(w_out.shape, lambda i: (0, 0))],
        out_specs=row_block(d),
        out_shape=jax.ShapeDtypeStruct((m, d), F32),
        compiler_params=_params(("arbitrary",)),
        name="out_proj",
    )(x, attn, pool, w_out)


def _ffn_kernel(x_ref, g_ref, wg_ref, wu_ref, wd_ref, gf_ref, o_ref, h_ref, *, final_norm):
    j = pl.program_id(1)

    @pl.when(j == 0)
    def _():
        xf = x_ref[...]
        h_ref[...] = _rmsnorm_rows(xf, g_ref[...]).astype(BF16)
        o_ref[...] = xf

    h = h_ref[...]
    gate = jnp.dot(h, wg_ref[...], preferred_element_type=F32)
    up = jnp.dot(h, wu_ref[...], preferred_element_type=F32)
    act = (gate * jax.nn.sigmoid(gate) * up).astype(BF16)
    o_ref[...] += jnp.dot(act, wd_ref[...], preferred_element_type=F32)

    if final_norm:
        @pl.when(j == pl.num_programs(1) - 1)
        def _():
            o_ref[...] = _rmsnorm_rows(o_ref[...], gf_ref[...])


def _ffn(x, g, w_gate, w_up, w_down, g_final, *, tm, tf, final_norm):
    m, d = x.shape
    f = w_gate.shape[1]
    assert m % tm == 0 and f % tf == 0
    return pl.pallas_call(
        functools.partial(_ffn_kernel, final_norm=final_norm),
        grid=(m // tm, f // tf),
        in_specs=[pl.BlockSpec((tm, d), lambda i, j: (i, 0)),
                  _resident((1, d), lambda i, j: (0, 0)),
                  pl.BlockSpec((d, tf), lambda i, j: (0, j)),
                  pl.BlockSpec((d, tf), lambda i, j: (0, j)),
                  pl.BlockSpec((tf, d), lambda i, j: (j, 0)),
                  _resident((1, d), lambda i, j: (0, 0))],
        out_specs=pl.BlockSpec((tm, d), lambda i, j: (i, 0)),
        out_shape=jax.ShapeDtypeStruct((m, d), F32),
        scratch_shapes=[pltpu.VMEM((tm, d), BF16)],
        compiler_params=_params(("arbitrary", "arbitrary")),
        name="ffn",
    )(x, g, w_gate, w_up, w_down, g_final)


def _t5_bucket(dist):
    max_exact = NUM_BUCKETS // 2
    df = jnp.maximum(dist, 1).astype(F32)
    large = max_exact + (jnp.log(df / max_exact) / math.log(MAX_WINDOW / max_exact)
                         * (NUM_BUCKETS - max_exact)).astype(jnp.int32)
    large = jnp.minimum(large, NUM_BUCKETS - 1)
    return jnp.where(dist < max_exact, dist, large)


def _prompt_bias_tables(rel_bias):
    period = 3 * BLOCK
    tabs = []
    for window, dil in DILATED_CONFIGS:
        sub_w = window // dil
        assert sub_w <= BLOCK
        bias_sub = rel_bias[_t5_bucket(dil * jnp.arange(sub_w + 1))].T.astype(F32) * LOG2E
        vec = jnp.full((N_HEADS, period), -jnp.inf, F32)
        vec = vec.at[:, BLOCK - sub_w:BLOCK + 1].set(bias_sub[:, ::-1])
        flat = jnp.tile(vec, (1, BLOCK))[:, :BLOCK * (period - 1)]
        tabs.append(flat.reshape(N_HEADS, BLOCK, period - 1)[:, :, :2 * BLOCK])
    return jnp.stack(tabs)


def _sample_tables(rel_bias, wc, t_new):
    tail = np.where(np.arange(BLOCK) < t_new, wc + np.arange(BLOCK), -1)
    key_pos = jnp.asarray(np.concatenate([np.arange(wc), tail]), jnp.int32)
    dist = wc + jnp.arange(t_new)[:, None] - key_pos[None, :]
    listed = (key_pos >= 0)[None, :] & (dist >= 0)
    cnt = sum((listed & (dist <= w) & (dist % d == 0)).astype(F32) for w, d in DILATED_CONFIGS)
    bias = rel_bias[_t5_bucket(jnp.maximum(dist, 0))].astype(F32) * LOG2E
    bias = jnp.where((cnt > 0)[..., None], bias, -jnp.inf)
    bias = bias.transpose(2, 0, 1).reshape(N_HEAD_TILES, HEADS_PER_TILE * t_new, key_pos.shape[0])
    return bias, jnp.tile(cnt, (HEADS_PER_TILE, 1))


def kernel(x_prompt, x_sample, cache_k, cache_v, state_pool, rel_bias, norm_mix, w_in, w_pool,
           pool_scale, w_out, norm_ffn, w_gate, w_up, w_down, norm_final):
    b, seq, d = x_prompt.shape
    db, t_new, _ = x_sample.shape
    depth = w_in.shape[0]
    wc = cache_k.shape[2]
    assert depth >= 1 and wc == MAX_WINDOW and seq >= MAX_WINDOW and t_new <= POOL_STATE

    tm_prompt = 512
    m_sample = db * t_new
    gf = norm_final.reshape(1, d)
    prompt_bias = _prompt_bias_tables(rel_bias)
    sample_bias, sample_cnt = _sample_tables(rel_bias, wc, t_new)
    cache_kt, cache_vt = (c.transpose(0, 1, 3, 4, 2) for c in (cache_k, cache_v))
    zero_hist = jnp.zeros((POOL_HIST, POOL_WIDTH), F32)

    def to_tok_major(a):
        return a.transpose(1, 0, 2).reshape(a.shape[1] * db, a.shape[2])

    def to_seq_major(a):
        return a.reshape(a.shape[0] // db, db, a.shape[1]).transpose(1, 0, 2)

    xp = x_prompt.reshape(b * seq, d)
    xs = to_tok_major(x_sample)
    outs = {name: [] for name in ("kp", "vp", "pp", "ks", "vs", "ps")}
    for l in range(depth):
        g_mix = norm_mix[l].reshape(1, d)
        g_ffn = norm_ffn[l].reshape(1, d)
        w_in_l = w_in[l].astype(BF16)
        w_pool_l = w_pool[l].astype(BF16)
        w_out_l = w_out[l].astype(BF16)
        w_gate_l, w_up_l, w_down_l = (w[l].astype(BF16) for w in (w_gate, w_up, w_down))
        scale_l = pool_scale[l].reshape(1, POOL_WIDTH)
        last = l == depth - 1

        tiles_per_seq = seq // tm_prompt
        q, k, v, kt, vt, pool, utail = _in_proj(
            xp, g_mix, w_in_l, zero_hist, w_pool_l, scale_l, tm=tm_prompt, tiles_per_seq=tiles_per_seq,
            shift=1, n_hist=0, tail_rows=POOL_HIST)
        attn = _attn_prompt(q.reshape(b, seq, -1), k.reshape(b, seq, -1), v.reshape(b, seq, -1),
                            prompt_bias)
        xp = _out_proj(xp, attn.reshape(b * seq, -1), pool, w_out_l, tm=tm_prompt)
        xp = _ffn(xp, g_ffn, w_gate_l, w_up_l, w_down_l, gf, tm=2 * tm_prompt, tf=512, final_norm=last)
        win = min(MAX_WINDOW, seq)
        outs["kp"].append(kt.transpose(0, 3, 1, 2)[:, -win:])
        outs["vp"].append(vt.transpose(0, 3, 1, 2)[:, -win:])
        outs["pp"].append(utail.reshape(b, tiles_per_seq, POOL_HIST, -1)[:, -1, -POOL_STATE:])

        hist = jnp.concatenate([jnp.zeros((db, POOL_WIDTH), F32), to_tok_major(state_pool[l])], axis=0)
        q, k, v, _, _, pool, utail = _in_proj(
            xs, g_mix, w_in_l, hist, w_pool_l, scale_l, tm=m_sample, tiles_per_seq=1, shift=db,
            n_hist=POOL_STATE, tail_rows=m_sample)
        q, k, v, u = (to_seq_major(a) for a in (q, k, v, utail[0]))
        attn = _attn_sample(q, k, v, cache_kt, cache_vt, sample_bias, sample_cnt, layer=l)
        xs = _out_proj(xs, to_tok_major(attn), pool, w_out_l, tm=m_sample)
        xs = _ffn(xs, g_ffn, w_gate_l, w_up_l, w_down_l, gf, tm=m_sample, tf=512, final_norm=last)
        outs["ks"].append(k.reshape(db, t_new, N_HEADS, HEAD_DIM))
        outs["vs"].append(v.reshape(db, t_new, N_HEADS, HEAD_DIM))
        outs["ps"].append(jnp.concatenate([state_pool[l], u], axis=1)[:, -POOL_STATE:])

    y_prompt = xp.reshape(b, seq, d)
    y_sample = to_seq_major(xs)
    return (y_prompt, y_sample, jnp.stack(outs["kp"]), jnp.stack(outs["vp"]), jnp.stack(outs["pp"]),
            jnp.stack(outs["ks"]), jnp.stack(outs["vs"]), jnp.stack(outs["ps"]))
```

```python
import functools
import math

import numpy as np

import jax
import jax.numpy as jnp
from jax import lax
from jax.experimental import pallas as pl
from jax.experimental.pallas import tpu as pltpu

HEAD_DIM = 64
N_HEADS = 16
ATTN_WIDTH = N_HEADS * HEAD_DIM
DILATED_CONFIGS = ((128, 1), (512, 4), (2048, 16))
MAX_WINDOW = 2048
BLOCK = 128
POOL_WINDOWS = (2, 4, 8, 16)
POOL_GROUP_DIM = 256
POOL_WIDTH = len(POOL_WINDOWS) * POOL_GROUP_DIM
POOL_STATE = max(POOL_WINDOWS) - 1
POOL_HIST = max(POOL_WINDOWS)
NUM_BUCKETS = 32
EPS = 1e-6
SCALE = HEAD_DIM ** -0.5
LOG2E = math.log2(math.e)

LANES = 128
HEADS_PER_TILE = LANES // HEAD_DIM
ATTN_GROUP = 4
N_HEAD_TILES = N_HEADS // HEADS_PER_TILE
VMEM_LIMIT = 56 * 1024 * 1024

F32 = jnp.float32
BF16 = jnp.bfloat16


def _params(semantics):
    return pltpu.CompilerParams(dimension_semantics=semantics, vmem_limit_bytes=VMEM_LIMIT)


def _resident(shape, index_map):
    return pl.BlockSpec(shape, index_map, pipeline_mode=pl.Buffered(1))


def _rmsnorm_rows(xf, g):
    return xf * lax.rsqrt(jnp.mean(xf * xf, axis=-1, keepdims=True) + EPS) * g


def _in_proj_kernel(x_ref, g_ref, w_ref, hist_ref, wp_ref, ps_ref,
                    q_ref, k_ref, v_ref, kt_ref, vt_ref, pool_ref, utail_ref,
                    h_ref, ucat_ref, pooled_ref, *, tm, tn, tiles_per_seq, shift, n_hist):
    seq_tile = pl.program_id(0) % tiles_per_seq
    hist_rows = POOL_HIST * shift
    tail_rows = utail_ref.shape[0]

    @pl.when(seq_tile == 0)
    def _():
        ucat_ref[0:hist_rows, :] = hist_ref[...]

    h_ref[...] = _rmsnorm_rows(x_ref[...], g_ref[...]).astype(BF16)
    per_out = ATTN_WIDTH // tn
    heads_per_chunk = tn // HEAD_DIM
    for c in list(range(3 * per_out, 4 * per_out)) + list(range(3 * per_out)):
        res = jnp.dot(h_ref[...], w_ref[:, c * tn:(c + 1) * tn], preferred_element_type=F32)
        which, part = divmod(c, per_out)
        cs = slice(part * tn, (part + 1) * tn)
        if which == 0:
            q_ref[:, cs] = res * (SCALE * LOG2E)
        elif which == 3:
            ucat_ref[hist_rows:hist_rows + tm, cs] = res
        else:
            nat, tr = ((k_ref, kt_ref), (v_ref, vt_ref))[which - 1]
            nat[:, cs] = res
            heads = slice(part * heads_per_chunk, (part + 1) * heads_per_chunk)
            tr[heads] = res.T.reshape(heads_per_chunk, HEAD_DIM, tm)

    utail_ref[...] = ucat_ref[hist_rows + tm - tail_rows:hist_rows + tm, :]

    chunk = min(128, tm)
    for r0 in range(0, tm, chunk):
        row = lax.broadcasted_iota(jnp.int32, (chunk, 1), 0)
        tok = (seq_tile * tm + r0 + row) // shift
        for g, w in enumerate(POOL_WINDOWS):
            cs = slice(g * POOL_GROUP_DIM, (g + 1) * POOL_GROUP_DIM)
            u0 = ucat_ref[hist_rows + r0:hist_rows + r0 + chunk, cs]
            acc = u0
            for j in range(1, w):
                lo = hist_rows + r0 - j * shift
                acc = acc + ucat_ref[lo:lo + chunk, cs]
            cnt = jnp.minimum(tok + (n_hist + 1), w).astype(F32)
            pooled_ref[r0:r0 + chunk, cs] = (acc / cnt - u0).astype(BF16)

    for g in range(len(POOL_WINDOWS)):
        cs = slice(g * POOL_GROUP_DIM, (g + 1) * POOL_GROUP_DIM)
        y = jnp.dot(pooled_ref[:, cs], wp_ref[g], preferred_element_type=F32) * ps_ref[:, cs]
        pool_ref[:, cs] = y.astype(BF16)

    if tiles_per_seq > 1:
        ucat_ref[0:hist_rows, :] = ucat_ref[tm:tm + hist_rows, :]


def _in_proj(x, g, w_in, hist, w_pool, pool_scale, *, tm, tiles_per_seq, shift, n_hist, tail_rows):
    m, d = x.shape
    in_cols = w_in.shape[1]
    hist_rows = POOL_HIST * shift
    n_tiles = m // tm
    assert m % tm == 0 and n_tiles % tiles_per_seq == 0 and hist.shape == (hist_rows, POOL_WIDTH)
    assert tiles_per_seq == 1 or tm >= hist_rows
    row_block = lambda width: pl.BlockSpec((tm, width), lambda i: (i, 0))
    tr_block = pl.BlockSpec((None, N_HEADS, HEAD_DIM, tm),
                            lambda i: (i // tiles_per_seq, 0, 0, i % tiles_per_seq))
    tr_shape = jax.ShapeDtypeStruct((n_tiles // tiles_per_seq, N_HEADS, HEAD_DIM, tiles_per_seq * tm), F32)
    kern = functools.partial(_in_proj_kernel, tm=tm, tn=512, tiles_per_seq=tiles_per_seq,
                             shift=shift, n_hist=n_hist)
    return pl.pallas_call(
        kern,
        grid=(n_tiles,),
        in_specs=[
            row_block(d),
            _resident((1, d), lambda i: (0, 0)),
            _resident((d, in_cols), lambda i: (0, 0)),
            _resident((hist_rows, POOL_WIDTH), lambda i: (0, 0)),
            _resident(w_pool.shape, lambda i: (0, 0, 0)),
            _resident((1, POOL_WIDTH), lambda i: (0, 0)),
        ],
        out_specs=[row_block(ATTN_WIDTH)] * 3 + [tr_block] * 2
        + [row_block(POOL_WIDTH), pl.BlockSpec((None, tail_rows, POOL_WIDTH), lambda i: (i, 0, 0))],
        out_shape=[jax.ShapeDtypeStruct((m, ATTN_WIDTH), F32)] * 3 + [tr_shape] * 2
        + [jax.ShapeDtypeStruct((m, POOL_WIDTH), BF16),
           jax.ShapeDtypeStruct((n_tiles, tail_rows, POOL_WIDTH), F32)],
        scratch_shapes=[
            pltpu.VMEM((tm, d), BF16),
            pltpu.VMEM((hist_rows + tm, POOL_WIDTH), F32),
            pltpu.VMEM((tm, POOL_WIDTH), BF16),
        ],
        compiler_params=_params(("arbitrary",)),
        name="in_proj",
    )(x, g, w_in, hist, w_pool, pool_scale)


def _attn_prompt_kernel(q_ref, k_ref, v_ref, bias_ref, wo_ref, wg_ref, wu_ref, wd_ref,
                        o_ref, wo_bf_ref, wg_bf_ref, wu_bf_ref, wd_bf_ref,
                        q4_ref, k4_ref, v4_ref, m4_ref, l4_ref, acc4_ref, m_ref, l_ref, acc_ref, *, seq):
    step = pl.program_id(0) * pl.num_programs(1) + pl.program_id(1)
    for src, dst in ((wo_ref, wo_bf_ref), (wg_ref, wg_bf_ref), (wu_ref, wu_bf_ref)):
        dst[...] = src[...].astype(BF16)

    @pl.when(step % 2 == 0)
    def _():
        wd_bf_ref[...] = wd_ref[...].astype(BF16)

    is_a = lax.broadcasted_iota(jnp.int32, (BLOCK, LANES), 1) < HEAD_DIM
    quarter = seq // 4
    nat_cfg, r4_cfg, r16_cfg = range(3)
    assert DILATED_CONFIGS[r4_cfg][1] == 4 and DILATED_CONFIGS[r16_cfg][1] == 16

    def for_chunks(fn):
        for r in range(4):
            for j in range(quarter // BLOCK):
                fn(pl.ds(r + 4 * BLOCK * j, BLOCK, stride=4), pl.ds(r * quarter + BLOCK * j, BLOCK))

    def to_residue_major(nat, r4):
        for src, dst in zip(nat, r4):
            def move(nat_rows, r4_rows, src=src, dst=dst):
                dst[r4_rows, :] = src[nat_rows, :]
            for_chunks(move)

    def to_natural(r4, nat):
        for src, dst in zip(r4, nat):
            def move(nat_rows, r4_rows, src=src, dst=dst):
                dst[nat_rows, :] = src[r4_rows, :]
            for_chunks(move)

    def compute(c, srcs, qr, kr, has_prev):
        q_src, k_src, v_src = srcs
        qb = q_src[qr, :]
        kb = k_src[kr, :].astype(BF16)
        vb = v_src[kr, :].astype(BF16)
        zero = jnp.zeros_like(qb)
        qs = jnp.concatenate([jnp.where(is_a, qb, zero), jnp.where(is_a, zero, qb)], axis=0).astype(BF16)
        bias = bias_ref[c] if has_prev else bias_ref[c, :, :, BLOCK:]
        lg = lax.dot_general(qs, kb, (((1,), (1,)), ((), ())), preferred_element_type=F32)
        lg = lg + bias.reshape(HEADS_PER_TILE * BLOCK, bias.shape[-1])
        m = jnp.max(lg, axis=-1, keepdims=True)
        p = jnp.exp2(lg - m)
        s = jnp.sum(p, axis=-1, keepdims=True)
        o = jnp.dot(p.astype(BF16), vb, preferred_element_type=F32)
        return tuple(jnp.where(is_a, x[:BLOCK], x[BLOCK:]) for x in (m, s, o))

    def rescale(m_old, m_new):
        e = jnp.exp2(-jnp.abs(m_old - m_new))
        keep = m_old >= m_new
        return jnp.where(keep, 1.0, e), jnp.where(keep, e, 1.0)

    def first_visit(state, qr, m_new, s_new, o_new):
        for ref, val in zip(state, (m_new, s_new, o_new)):
            ref[qr, :] = val

    def merge(state, qr, m_new, s_new, o_new):
        m_st, l_st, acc_st = state
        m_old = m_st[qr, :]
        a, b = rescale(m_old, m_new)
        m_st[qr, :] = jnp.maximum(m_old, m_new)
        l_st[qr, :] = l_st[qr, :] * a + s_new * b
        acc_st[qr, :] = acc_st[qr, :] * a + o_new * b

    def last_visit(state, qr, m_new, s_new, o_new):
        m_st, l_st, acc_st = state
        a, b = rescale(m_st[qr, :], m_new)
        o_ref[qr, :] = ((acc_st[qr, :] * a + o_new * b) / (l_st[qr, :] * a + s_new * b)).astype(o_ref.dtype)

    def run(c, srcs, units, visit, state):
        for g0 in range(0, len(units), ATTN_GROUP):
            group = units[g0:g0 + ATTN_GROUP]
            done = [compute(c, srcs, *u) for u in group]
            for (qr, _, _), d in zip(group, done):
                visit(state, qr, *d)

    def block_units(base, n_blocks):
        return [(pl.ds(base + n * BLOCK, BLOCK),
                 pl.ds(base + (n - 1) * BLOCK, 2 * BLOCK) if n else pl.ds(base, BLOCK), n > 0)
                for n in range(n_blocks)]

    nat_src, r4_src = (q_ref, k_ref, v_ref), (q4_ref, k4_ref, v4_ref)
    nat_state, r4_state = (m_ref, l_ref, acc_ref), (m4_ref, l4_ref, acc4_ref)
    to_residue_major(nat_src, r4_src)

    units = [u for r in range(4) for u in block_units(r * quarter, quarter // BLOCK)]
    run(r4_cfg, r4_src, units, first_visit, r4_state)

    assert seq == 16 * BLOCK
    units = []
    for r16 in range(16):
        rows16 = pl.ds((r16 % 4) * quarter + r16 // 4, BLOCK, stride=4)
        units.append((rows16, rows16, False))
    run(r16_cfg, r4_src, units, merge, r4_state)

    to_natural(r4_state, nat_state)
    run(nat_cfg, nat_src, block_units(0, seq // BLOCK), last_visit, nat_state)


def _attn_prompt(q, k, v, bias, w_out, w_gate, w_up, w_down, *, layer):
    b, seq, _ = q.shape
    assert all(seq % (BLOCK * dil) == 0 for _, dil in DILATED_CONFIGS)
    steps = N_HEAD_TILES * b
    bf16_rows = 16
    blk = pl.BlockSpec((None, seq, LANES), lambda t, i: (i, 0, t))

    def slab(w, every):
        rows = w.shape[1] * every // steps
        assert w.shape[1] * every % steps == 0 and rows % bf16_rows == 0
        return pl.BlockSpec((None, rows, w.shape[2]), lambda t, i: (layer, (t * b + i) // every, 0))

    def slab_out(w, every):
        rows = w.shape[1] * every // steps
        return pl.BlockSpec((rows, w.shape[2]), lambda t, i: ((t * b + i) // every, 0))

    weights = ((w_out, 1), (w_gate, 1), (w_up, 1), (w_down, 2))
    return pl.pallas_call(
        functools.partial(_attn_prompt_kernel, seq=seq),
        grid=(N_HEAD_TILES, b),
        in_specs=[blk, blk, blk,
                  pl.BlockSpec((len(DILATED_CONFIGS), HEADS_PER_TILE, BLOCK, 2 * BLOCK),
                               lambda t, i: (0, t, 0, 0))] + [slab(w, every) for w, every in weights],
        out_specs=[blk] + [slab_out(w, every) for w, every in weights],
        out_shape=[jax.ShapeDtypeStruct((b, seq, ATTN_WIDTH), BF16)]
        + [jax.ShapeDtypeStruct(w.shape[1:], BF16) for w, _ in weights],
        scratch_shapes=[pltpu.VMEM((seq, LANES), F32)] * 9,
        compiler_params=_params(("arbitrary", "arbitrary")),
        name="attn_prompt",
    )(q, k, v, bias, *(w for w, _ in weights))


def _attn_sample_kernel(q_ref, kn_ref, vn_ref, bias_ref, cnt_ref, kt_ref, vt_ref, o_ref,
                        ktail_ref, vtail_ref, *, t_new, wc):
    @pl.when(pl.program_id(0) == 0)
    def _():
        ktail_ref[...] = jnp.zeros_like(ktail_ref)
        vtail_ref[...] = jnp.zeros_like(vtail_ref)

    ktail_ref[0:t_new, :] = kn_ref[...]
    vtail_ref[0:t_new, :] = vn_ref[...]
    q = q_ref[...]
    first = lax.broadcasted_iota(jnp.int32, (HEADS_PER_TILE * t_new, 1), 0) < t_new
    contract_last = (((1,), (1,)), ((), ()))
    cnt_cache, cnt_new = cnt_ref[:, :wc], cnt_ref[:, wc:]
    for j in range(N_HEAD_TILES):
        pair = (HEADS_PER_TILE * j, HEADS_PER_TILE * j + 1)
        cols = [slice(h * HEAD_DIM, (h + 1) * HEAD_DIM) for h in pair]
        lhs = jnp.concatenate([q[:, cs] for cs in cols], axis=0).astype(BF16)
        own = lambda a, b: jnp.where(first, a, b)
        lc = own(*(jnp.dot(lhs, kt_ref[h].astype(BF16), preferred_element_type=F32) for h in pair))
        ln = own(*(lax.dot_general(lhs, ktail_ref[:, cs].astype(BF16), contract_last,
                                   preferred_element_type=F32) for cs in cols))
        lc = lc + bias_ref[j, :, :wc]
        ln = ln + bias_ref[j, :, wc:]
        m = jnp.maximum(jnp.max(lc, axis=-1, keepdims=True), jnp.max(ln, axis=-1, keepdims=True))
        pc = cnt_cache * jnp.exp2(lc - m)
        pn = cnt_new * jnp.exp2(ln - m)
        s = jnp.sum(pc, axis=-1, keepdims=True) + jnp.sum(pn, axis=-1, keepdims=True)
        pc, pn = pc.astype(BF16), pn.astype(BF16)
        for h, cs, rows in zip(pair, cols, (slice(0, t_new), slice(t_new, 2 * t_new))):
            o = (lax.dot_general(pc, vt_ref[h].astype(BF16), contract_last, preferred_element_type=F32)
                 + jnp.dot(pn, vtail_ref[:, cs].astype(BF16), preferred_element_type=F32)) / s
            o_ref[:, cs] = o[rows]


def _attn_sample(q, kn, vn, cache_kt, cache_vt, bias, cnt, *, layer):
    db, t_new, width = q.shape
    wc = cache_kt.shape[-1]
    nk = wc + BLOCK
    assert HEADS_PER_TILE * t_new == 8 and bias.shape == (N_HEAD_TILES, HEADS_PER_TILE * t_new, nk)
    assert cache_kt.shape[1:] == (db, N_HEADS, HEAD_DIM, wc)
    new_blk = pl.BlockSpec((None, t_new, width), lambda i: (i, 0, 0))
    cache_blk = pl.BlockSpec((None, None, N_HEADS, HEAD_DIM, wc), lambda i: (layer, i, 0, 0, 0))
    return pl.pallas_call(
        functools.partial(_attn_sample_kernel, t_new=t_new, wc=wc),
        grid=(db,),
        in_specs=[new_blk, new_blk, new_blk,
                  _resident(bias.shape, lambda i: (0, 0, 0)), _resident(cnt.shape, lambda i: (0, 0)),
                  cache_blk, cache_blk],
        out_specs=new_blk,
        out_shape=jax.ShapeDtypeStruct((db, t_new, width), F32),
        scratch_shapes=[pltpu.VMEM((BLOCK, width), F32)] * 2,
        compiler_params=_params(("arbitrary",)),
        name="attn_sample",
    )(q, kn, vn, bias, cnt, cache_kt, cache_vt)


def _out_proj_kernel(x_ref, a_ref, p_ref, w_ref, o_ref, *, tn):
    a = a_ref[...].astype(BF16)
    p = p_ref[...]
    for c in range(o_ref.shape[1] // tn):
        cs = slice(c * tn, (c + 1) * tn)
        mixed = (jnp.dot(a, w_ref[0:ATTN_WIDTH, cs], preferred_element_type=F32)
                 + jnp.dot(p, w_ref[ATTN_WIDTH:, cs], preferred_element_type=F32))
        o_ref[:, cs] = x_ref[:, cs] + mixed


def _out_proj(x, attn, pool, w_out, *, tm):
    m, d = x.shape
    row_block = lambda width: pl.BlockSpec((tm, width), lambda i: (i, 0))
    return pl.pallas_call(
        functools.partial(_out_proj_kernel, tn=512),
        grid=(m // tm,),
        in_specs=[row_block(d), row_block(ATTN_WIDTH), row_block(POOL_WIDTH),
                  _resident(w_out.shape, lambda i: (0, 0))],
        out_specs=row_block(d),
        out_shape=jax.ShapeDtypeStruct((m, d), F32),
        compiler_params=_params(("arbitrary",)),
        name="out_proj",
    )(x, attn, pool, w_out)


def _ffn_kernel(x_ref, g_ref, wg_ref, wu_ref, wd_ref, gf_ref, o_ref, h_ref, *, final_norm):
    j = pl.program_id(1)

    @pl.when(j == 0)
    def _():
        xf = x_ref[...]
        h_ref[...] = _rmsnorm_rows(xf, g_ref[...]).astype(BF16)
        o_ref[...] = xf

    h = h_ref[...]
    gate = jnp.dot(h, wg_ref[...], preferred_element_type=F32)
    up = jnp.dot(h, wu_ref[...], preferred_element_type=F32)
    act = (gate * jax.nn.sigmoid(gate) * up).astype(BF16)
    o_ref[...] += jnp.dot(act, wd_ref[...], preferred_element_type=F32)

    if final_norm:
        @pl.when(j == pl.num_programs(1) - 1)
        def _():
            o_ref[...] = _rmsnorm_rows(o_ref[...], gf_ref[...])


def _ffn(x, g, w_gate, w_up, w_down, g_final, *, tm, tf, final_norm):
    m, d = x.shape
    f = w_gate.shape[1]
    assert m % tm == 0 and f % tf == 0
    return pl.pallas_call(
        functools.partial(_ffn_kernel, final_norm=final_norm),
        grid=(m // tm, f // tf),
        in_specs=[pl.BlockSpec((tm, d), lambda i, j: (i, 0)),
                  _resident((1, d), lambda i, j: (0, 0)),
                  pl.BlockSpec((d, tf), lambda i, j: (0, j)),
                  pl.BlockSpec((d, tf), lambda i, j: (0, j)),
                  pl.BlockSpec((tf, d), lambda i, j: (j, 0)),
                  _resident((1, d), lambda i, j: (0, 0))],
        out_specs=pl.BlockSpec((tm, d), lambda i, j: (i, 0)),
        out_shape=jax.ShapeDtypeStruct((m, d), F32),
        scratch_shapes=[pltpu.VMEM((tm, d), BF16)],
        compiler_params=_params(("arbitrary", "arbitrary")),
        name="ffn",
    )(x, g, w_gate, w_up, w_down, g_final)


def _t5_bucket(dist):
    max_exact = NUM_BUCKETS // 2
    df = jnp.maximum(dist, 1).astype(F32)
    large = max_exact + (jnp.log(df / max_exact) / math.log(MAX_WINDOW / max_exact)
                         * (NUM_BUCKETS - max_exact)).astype(jnp.int32)
    large = jnp.minimum(large, NUM_BUCKETS - 1)
    return jnp.where(dist < max_exact, dist, large)


def _prompt_bias_tables(rel_bias):
    period = 3 * BLOCK
    tabs = []
    for window, dil in DILATED_CONFIGS:
        sub_w = window // dil
        assert sub_w <= BLOCK
        bias_sub = rel_bias[_t5_bucket(dil * jnp.arange(sub_w + 1))].T.astype(F32) * LOG2E
        vec = jnp.full((N_HEADS, period), -jnp.inf, F32)
        vec = vec.at[:, BLOCK - sub_w:BLOCK + 1].set(bias_sub[:, ::-1])
        flat = jnp.tile(vec, (1, BLOCK))[:, :BLOCK * (period - 1)]
        tabs.append(flat.reshape(N_HEADS, BLOCK, period - 1)[:, :, :2 * BLOCK])
    return jnp.stack(tabs)


def _sample_tables(rel_bias, wc, t_new):
    tail = np.where(np.arange(BLOCK) < t_new, wc + np.arange(BLOCK), -1)
    key_pos = jnp.asarray(np.concatenate([np.arange(wc), tail]), jnp.int32)
    dist = wc + jnp.arange(t_new)[:, None] - key_pos[None, :]
    listed = (key_pos >= 0)[None, :] & (dist >= 0)
    cnt = sum((listed & (dist <= w) & (dist % d == 0)).astype(F32) for w, d in DILATED_CONFIGS)
    bias = rel_bias[_t5_bucket(jnp.maximum(dist, 0))].astype(F32) * LOG2E
    bias = jnp.where((cnt > 0)[..., None], bias, -jnp.inf)
    bias = bias.transpose(2, 0, 1).reshape(N_HEAD_TILES, HEADS_PER_TILE * t_new, key_pos.shape[0])
    return bias, jnp.tile(cnt, (HEADS_PER_TILE, 1))


def kernel(x_prompt, x_sample, cache_k, cache_v, state_pool, rel_bias, norm_mix, w_in, w_pool,
           pool_scale, w_out, norm_ffn, w_gate, w_up, w_down, norm_final):
    b, seq, d = x_prompt.shape
    db, t_new, _ = x_sample.shape
    depth = w_in.shape[0]
    wc = cache_k.shape[2]
    assert depth >= 1 and wc == MAX_WINDOW and seq >= MAX_WINDOW and t_new <= POOL_STATE

    tm_prompt = 512
    m_sample = db * t_new
    gf = norm_final.reshape(1, d)
    prompt_bias = _prompt_bias_tables(rel_bias)
    sample_bias, sample_cnt = _sample_tables(rel_bias, wc, t_new)
    cache_kt, cache_vt = (c.transpose(0, 1, 3, 4, 2) for c in (cache_k, cache_v))
    zero_hist = jnp.zeros((POOL_HIST, POOL_WIDTH), F32)

    def to_tok_major(a):
        return a.transpose(1, 0, 2).reshape(a.shape[1] * db, a.shape[2])

    def to_seq_major(a):
        return a.reshape(a.shape[0] // db, db, a.shape[1]).transpose(1, 0, 2)

    xp = x_prompt.reshape(b * seq, d)
    xs = to_tok_major(x_sample)
    outs = {name: [] for name in ("kp", "vp", "pp", "ks", "vs", "ps")}
    for l in range(depth):
        g_mix = norm_mix[l].reshape(1, d)
        g_ffn = norm_ffn[l].reshape(1, d)
        w_in_l = w_in[l].astype(BF16)
        w_pool_l = w_pool[l].astype(BF16)
        scale_l = pool_scale[l].reshape(1, POOL_WIDTH)
        last = l == depth - 1

        tiles_per_seq = seq // tm_prompt
        q, k, v, kt, vt, pool, utail = _in_proj(
            xp, g_mix, w_in_l, zero_hist, w_pool_l, scale_l, tm=tm_prompt, tiles_per_seq=tiles_per_seq,
            shift=1, n_hist=0, tail_rows=POOL_HIST)
        attn, w_out_l, w_gate_l, w_up_l, w_down_l = _attn_prompt(
            q.reshape(b, seq, -1), k.reshape(b, seq, -1), v.reshape(b, seq, -1), prompt_bias,
            w_out, w_gate, w_up, w_down, layer=l)
        xp = _out_proj(xp, attn.reshape(b * seq, -1), pool, w_out_l, tm=tm_prompt)
        xp = _ffn(xp, g_ffn, w_gate_l, w_up_l, w_down_l, gf, tm=2 * tm_prompt, tf=512, final_norm=last)
        win = min(MAX_WINDOW, seq)
        outs["kp"].append(kt.transpose(0, 3, 1, 2)[:, -win:])
        outs["vp"].append(vt.transpose(0, 3, 1, 2)[:, -win:])
        outs["pp"].append(utail.reshape(b, tiles_per_seq, POOL_HIST, -1)[:, -1, -POOL_STATE:])

        hist = jnp.concatenate([jnp.zeros((db, POOL_WIDTH), F32), to_tok_major(state_pool[l])], axis=0)
        q, k, v, _, _, pool, utail = _in_proj(
            xs, g_mix, w_in_l, hist, w_pool_l, scale_l, tm=m_sample, tiles_per_seq=1, shift=db,
            n_hist=POOL_STATE, tail_rows=m_sample)
        q, k, v, u = (to_seq_major(a) for a in (q, k, v, utail[0]))
        attn = _attn_sample(q, k, v, cache_kt, cache_vt, sample_bias, sample_cnt, layer=l)
        xs = _out_proj(xs, to_tok_major(attn), pool, w_out_l, tm=m_sample)
        xs = _ffn(xs, g_ffn, w_gate_l, w_up_l, w_down_l, gf, tm=m_sample, tf=512, final_norm=last)
        outs["ks"].append(k.reshape(db, t_new, N_HEADS, HEAD_DIM))
        outs["vs"].append(v.reshape(db, t_new, N_HEADS, HEAD_DIM))
        outs["ps"].append(jnp.concatenate([state_pool[l], u], axis=1)[:, -POOL_STATE:])

    y_prompt = xp.reshape(b, seq, d)
    y_sample = to_seq_major(xs)
    return (y_prompt, y_sample, jnp.stack(outs["kp"]), jnp.stack(outs["vp"]), jnp.stack(outs["pp"]),
            jnp.stack(outs["ks"]), jnp.stack(outs["vs"]), jnp.stack(outs["ps"]))
```

```python
import functools
import math

import numpy as np

import jax
import jax.numpy as jnp
from jax import lax
from jax.experimental import pallas as pl
from jax.experimental.pallas import tpu as pltpu

HEAD_DIM = 64
N_HEADS = 16
ATTN_WIDTH = N_HEADS * HEAD_DIM
DILATED_CONFIGS = ((128, 1), (512, 4), (2048, 16))
MAX_WINDOW = 2048
BLOCK = 128
POOL_WINDOWS = (2, 4, 8, 16)
POOL_GROUP_DIM = 256
POOL_WIDTH = len(POOL_WINDOWS) * POOL_GROUP_DIM
POOL_STATE = max(POOL_WINDOWS) - 1
POOL_HIST = max(POOL_WINDOWS)
NUM_BUCKETS = 32
EPS = 1e-6
SCALE = HEAD_DIM ** -0.5
LOG2E = math.log2(math.e)

LANES = 128
HEADS_PER_TILE = LANES // HEAD_DIM
ATTN_GROUP = 4
N_HEAD_TILES = N_HEADS // HEADS_PER_TILE
VMEM_LIMIT = 56 * 1024 * 1024

F32 = jnp.float32
BF16 = jnp.bfloat16


def _params(semantics):
    return pltpu.CompilerParams(dimension_semantics=semantics, vmem_limit_bytes=VMEM_LIMIT)


def _resident(shape, index_map):
    return pl.BlockSpec(shape, index_map, pipeline_mode=pl.Buffered(1))


def _rmsnorm_rows(xf, g):
    return xf * lax.rsqrt(jnp.mean(xf * xf, axis=-1, keepdims=True) + EPS) * g


def _in_proj_kernel(x_ref, g_ref, w_ref, hist_ref, wp_ref, ps_ref,
                    q_ref, k_ref, v_ref, kt_ref, vt_ref, pool_ref, utail_ref,
                    h_ref, ucat_ref, pooled_ref, *, tm, tn, tiles_per_seq, shift, n_hist):
    seq_tile = pl.program_id(0) % tiles_per_seq
    hist_rows = POOL_HIST * shift
    tail_rows = utail_ref.shape[0]

    @pl.when(seq_tile == 0)
    def _():
        ucat_ref[0:hist_rows, :] = hist_ref[...]

    h_ref[...] = _rmsnorm_rows(x_ref[...], g_ref[...]).astype(BF16)
    per_out = ATTN_WIDTH // tn
    heads_per_chunk = tn // HEAD_DIM
    for c in list(range(3 * per_out, 4 * per_out)) + list(range(3 * per_out)):
        res = jnp.dot(h_ref[...], w_ref[:, c * tn:(c + 1) * tn], preferred_element_type=F32)
        which, part = divmod(c, per_out)
        cs = slice(part * tn, (part + 1) * tn)
        if which == 0:
            q_ref[:, cs] = res * (SCALE * LOG2E)
        elif which == 3:
            ucat_ref[hist_rows:hist_rows + tm, cs] = res
        else:
            nat, tr = ((k_ref, kt_ref), (v_ref, vt_ref))[which - 1]
            nat[:, cs] = res
            heads = slice(part * heads_per_chunk, (part + 1) * heads_per_chunk)
            tr[heads] = res.T.reshape(heads_per_chunk, HEAD_DIM, tm)

    utail_ref[...] = ucat_ref[hist_rows + tm - tail_rows:hist_rows + tm, :]

    chunk = min(128, tm)
    for r0 in range(0, tm, chunk):
        row = lax.broadcasted_iota(jnp.int32, (chunk, 1), 0)
        tok = (seq_tile * tm + r0 + row) // shift
        for g, w in enumerate(POOL_WINDOWS):
            cs = slice(g * POOL_GROUP_DIM, (g + 1) * POOL_GROUP_DIM)
            u0 = ucat_ref[hist_rows + r0:hist_rows + r0 + chunk, cs]
            acc = u0
            for j in range(1, w):
                lo = hist_rows + r0 - j * shift
                acc = acc + ucat_ref[lo:lo + chunk, cs]
            cnt = jnp.minimum(tok + (n_hist + 1), w).astype(F32)
            pooled_ref[r0:r0 + chunk, cs] = (acc / cnt - u0).astype(BF16)

    for g in range(len(POOL_WINDOWS)):
        cs = slice(g * POOL_GROUP_DIM, (g + 1) * POOL_GROUP_DIM)
        y = jnp.dot(pooled_ref[:, cs], wp_ref[g], preferred_element_type=F32) * ps_ref[:, cs]
        pool_ref[:, cs] = y.astype(BF16)

    if tiles_per_seq > 1:
        ucat_ref[0:hist_rows, :] = ucat_ref[tm:tm + hist_rows, :]


def _in_proj(x, g, w_in, hist, w_pool, pool_scale, *, tm, tiles_per_seq, shift, n_hist, tail_rows):
    m, d = x.shape
    in_cols = w_in.shape[1]
    hist_rows = POOL_HIST * shift
    n_tiles = m // tm
    assert m % tm == 0 and n_tiles % tiles_per_seq == 0 and hist.shape == (hist_rows, POOL_WIDTH)
    assert tiles_per_seq == 1 or tm >= hist_rows
    row_block = lambda width: pl.BlockSpec((tm, width), lambda i: (i, 0))
    tr_block = pl.BlockSpec((None, N_HEADS, HEAD_DIM, tm),
                            lambda i: (i // tiles_per_seq, 0, 0, i % tiles_per_seq))
    tr_shape = jax.ShapeDtypeStruct((n_tiles // tiles_per_seq, N_HEADS, HEAD_DIM, tiles_per_seq * tm), F32)
    kern = functools.partial(_in_proj_kernel, tm=tm, tn=512, tiles_per_seq=tiles_per_seq,
                             shift=shift, n_hist=n_hist)
    return pl.pallas_call(
        kern,
        grid=(n_tiles,),
        in_specs=[
            row_block(d),
            _resident((1, d), lambda i: (0, 0)),
            _resident((d, in_cols), lambda i: (0, 0)),
            _resident((hist_rows, POOL_WIDTH), lambda i: (0, 0)),
            _resident(w_pool.shape, lambda i: (0, 0, 0)),
            _resident((1, POOL_WIDTH), lambda i: (0, 0)),
        ],
        out_specs=[row_block(ATTN_WIDTH)] * 3 + [tr_block] * 2
        + [row_block(POOL_WIDTH), pl.BlockSpec((None, tail_rows, POOL_WIDTH), lambda i: (i, 0, 0))],
        out_shape=[jax.ShapeDtypeStruct((m, ATTN_WIDTH), F32)] * 3 + [tr_shape] * 2
        + [jax.ShapeDtypeStruct((m, POOL_WIDTH), BF16),
           jax.ShapeDtypeStruct((n_tiles, tail_rows, POOL_WIDTH), F32)],
        scratch_shapes=[
            pltpu.VMEM((tm, d), BF16),
            pltpu.VMEM((hist_rows + tm, POOL_WIDTH), F32),
            pltpu.VMEM((tm, POOL_WIDTH), BF16),
        ],
        compiler_params=_params(("arbitrary",)),
        name="in_proj",
    )(x, g, w_in, hist, w_pool, pool_scale)


def _attn_prompt_kernel(q_ref, k_ref, v_ref, bvec_ref, wo_ref, wg_ref, wu_ref, wd_ref,
                        o_ref, wo_bf_ref, wg_bf_ref, wu_bf_ref, wd_bf_ref, bias_ref,
                        q4_ref, k4_ref, v4_ref, m4_ref, l4_ref, acc4_ref, m_ref, l_ref, acc_ref, *, seq):
    step = pl.program_id(0) * pl.num_programs(1) + pl.program_id(1)
    for src, dst in ((wo_ref, wo_bf_ref), (wg_ref, wg_bf_ref), (wu_ref, wu_bf_ref)):
        dst[...] = src[...].astype(BF16)

    @pl.when(step % 2 == 0)
    def _():
        wd_bf_ref[...] = wd_ref[...].astype(BF16)

    @pl.when(pl.program_id(1) == 0)
    def _():
        for c in range(len(DILATED_CONFIGS)):
            for hh in range(HEADS_PER_TILE):
                vec = bvec_ref[c * HEADS_PER_TILE + hh:c * HEADS_PER_TILE + hh + 1, :]
                rows = jnp.broadcast_to(vec, (BLOCK, BIAS_PERIOD))
                bias_ref[c, hh] = pltpu.roll(rows, 0, 1, stride=1, stride_axis=0)[:, :2 * BLOCK]

    is_a = lax.broadcasted_iota(jnp.int32, (BLOCK, LANES), 1) < HEAD_DIM
    quarter = seq // 4
    nat_cfg, r4_cfg, r16_cfg = range(3)
    assert DILATED_CONFIGS[r4_cfg][1] == 4 and DILATED_CONFIGS[r16_cfg][1] == 16

    def for_chunks(fn):
        for r in range(4):
            for j in range(quarter // BLOCK):
                fn(pl.ds(r + 4 * BLOCK * j, BLOCK, stride=4), pl.ds(r * quarter + BLOCK * j, BLOCK))

    def to_residue_major(nat, r4):
        for src, dst in zip(nat, r4):
            def move(nat_rows, r4_rows, src=src, dst=dst):
                dst[r4_rows, :] = src[nat_rows, :]
            for_chunks(move)

    def to_natural(r4, nat):
        for src, dst in zip(r4, nat):
            def move(nat_rows, r4_rows, src=src, dst=dst):
                dst[nat_rows, :] = src[r4_rows, :]
            for_chunks(move)

    def compute(c, srcs, qr, kr, has_prev):
        q_src, k_src, v_src = srcs
        qb = q_src[qr, :]
        kb = k_src[kr, :].astype(BF16)
        vb = v_src[kr, :].astype(BF16)
        zero = jnp.zeros_like(qb)
        qs = jnp.concatenate([jnp.where(is_a, qb, zero), jnp.where(is_a, zero, qb)], axis=0).astype(BF16)
        bias = bias_ref[c] if has_prev else bias_ref[c, :, :, BLOCK:]
        lg = lax.dot_general(qs, kb, (((1,), (1,)), ((), ())), preferred_element_type=F32)
        lg = lg + bias.reshape(HEADS_PER_TILE * BLOCK, bias.shape[-1])
        m = jnp.max(lg, axis=-1, keepdims=True)
        p = jnp.exp2(lg - m)
        s = jnp.sum(p, axis=-1, keepdims=True)
        o = jnp.dot(p.astype(BF16), vb, preferred_element_type=F32)
        return tuple(jnp.where(is_a, x[:BLOCK], x[BLOCK:]) for x in (m, s, o))

    def rescale(m_old, m_new):
        e = jnp.exp2(-jnp.abs(m_old - m_new))
        keep = m_old >= m_new
        return jnp.where(keep, 1.0, e), jnp.where(keep, e, 1.0)

    def first_visit(state, qr, m_new, s_new, o_new):
        for ref, val in zip(state, (m_new, s_new, o_new)):
            ref[qr, :] = val

    def merge(state, qr, m_new, s_new, o_new):
        m_st, l_st, acc_st = state
        m_old = m_st[qr, :]
        a, b = rescale(m_old, m_new)
        m_st[qr, :] = jnp.maximum(m_old, m_new)
        l_st[qr, :] = l_st[qr, :] * a + s_new * b
        acc_st[qr, :] = acc_st[qr, :] * a + o_new * b

    def last_visit(state, qr, m_new, s_new, o_new):
        m_st, l_st, acc_st = state
        a, b = rescale(m_st[qr, :], m_new)
        o_ref[qr, :] = ((acc_st[qr, :] * a + o_new * b) / (l_st[qr, :] * a + s_new * b)).astype(o_ref.dtype)

    def run(c, srcs, units, visit, state):
        for g0 in range(0, len(units), ATTN_GROUP):
            group = units[g0:g0 + ATTN_GROUP]
            done = [compute(c, srcs, *u) for u in group]
            for (qr, _, _), d in zip(group, done):
                visit(state, qr, *d)

    def block_units(base, n_blocks):
        return [(pl.ds(base + n * BLOCK, BLOCK),
                 pl.ds(base + (n - 1) * BLOCK, 2 * BLOCK) if n else pl.ds(base, BLOCK), n > 0)
                for n in range(n_blocks)]

    nat_src, r4_src = (q_ref, k_ref, v_ref), (q4_ref, k4_ref, v4_ref)
    nat_state, r4_state = (m_ref, l_ref, acc_ref), (m4_ref, l4_ref, acc4_ref)
    to_residue_major(nat_src, r4_src)

    units = [u for r in range(4) for u in block_units(r * quarter, quarter // BLOCK)]
    run(r4_cfg, r4_src, units, first_visit, r4_state)

    assert seq == 16 * BLOCK
    units = []
    for r16 in range(16):
        rows16 = pl.ds((r16 % 4) * quarter + r16 // 4, BLOCK, stride=4)
        units.append((rows16, rows16, False))
    run(r16_cfg, r4_src, units, merge, r4_state)

    to_natural(r4_state, nat_state)
    run(nat_cfg, nat_src, block_units(0, seq // BLOCK), last_visit, nat_state)


def _attn_prompt(q, k, v, bias_vecs, w_out, w_gate, w_up, w_down, *, layer):
    b, seq, _ = q.shape
    assert all(seq % (BLOCK * dil) == 0 for _, dil in DILATED_CONFIGS)
    steps = N_HEAD_TILES * b
    bf16_rows = 16
    blk = pl.BlockSpec((None, seq, LANES), lambda t, i: (i, 0, t))

    def slab(w, every):
        rows = w.shape[1] * every // steps
        assert w.shape[1] * every % steps == 0 and rows % bf16_rows == 0
        return pl.BlockSpec((None, rows, w.shape[2]), lambda t, i: (layer, (t * b + i) // every, 0))

    def slab_out(w, every):
        rows = w.shape[1] * every // steps
        return pl.BlockSpec((rows, w.shape[2]), lambda t, i: ((t * b + i) // every, 0))

    weights = ((w_out, 1), (w_gate, 1), (w_up, 1), (w_down, 2))
    return pl.pallas_call(
        functools.partial(_attn_prompt_kernel, seq=seq),
        grid=(N_HEAD_TILES, b),
        in_specs=[blk, blk, blk,
                  pl.BlockSpec((None,) + bias_vecs.shape[1:], lambda t, i: (t, 0, 0))]
        + [slab(w, every) for w, every in weights],
        out_specs=[blk] + [slab_out(w, every) for w, every in weights],
        out_shape=[jax.ShapeDtypeStruct((b, seq, ATTN_WIDTH), BF16)]
        + [jax.ShapeDtypeStruct(w.shape[1:], BF16) for w, _ in weights],
        scratch_shapes=[pltpu.VMEM((len(DILATED_CONFIGS), HEADS_PER_TILE, BLOCK, 2 * BLOCK), F32)]
        + [pltpu.VMEM((seq, LANES), F32)] * 9,
        compiler_params=_params(("arbitrary", "arbitrary")),
        name="attn_prompt",
    )(q, k, v, bias_vecs, *(w for w, _ in weights))


def _attn_sample_kernel(q_ref, kn_ref, vn_ref, bias_ref, cnt_ref, kt_ref, vt_ref, o_ref,
                        ktail_ref, vtail_ref, *, t_new, wc):
    @pl.when(pl.program_id(0) == 0)
    def _():
        ktail_ref[...] = jnp.zeros_like(ktail_ref)
        vtail_ref[...] = jnp.zeros_like(vtail_ref)

    ktail_ref[0:t_new, :] = kn_ref[...]
    vtail_ref[0:t_new, :] = vn_ref[...]
    q = q_ref[...]
    first = lax.broadcasted_iota(jnp.int32, (HEADS_PER_TILE * t_new, 1), 0) < t_new
    contract_last = (((1,), (1,)), ((), ()))
    cnt_cache, cnt_new = cnt_ref[:, :wc], cnt_ref[:, wc:]
    for j in range(N_HEAD_TILES):
        pair = (HEADS_PER_TILE * j, HEADS_PER_TILE * j + 1)
        cols = [slice(h * HEAD_DIM, (h + 1) * HEAD_DIM) for h in pair]
        lhs = jnp.concatenate([q[:, cs] for cs in cols], axis=0).astype(BF16)
        own = lambda a, b: jnp.where(first, a, b)
        lc = own(*(jnp.dot(lhs, kt_ref[h].astype(BF16), preferred_element_type=F32) for h in pair))
        ln = own(*(lax.dot_general(lhs, ktail_ref[:, cs].astype(BF16), contract_last,
                                   preferred_element_type=F32) for cs in cols))
        lc = lc + bias_ref[j, :, :wc]
        ln = ln + bias_ref[j, :, wc:]
        m = jnp.maximum(jnp.max(lc, axis=-1, keepdims=True), jnp.max(ln, axis=-1, keepdims=True))
        pc = cnt_cache * jnp.exp2(lc - m)
        pn = cnt_new * jnp.exp2(ln - m)
        s = jnp.sum(pc, axis=-1, keepdims=True) + jnp.sum(pn, axis=-1, keepdims=True)
        pc, pn = pc.astype(BF16), pn.astype(BF16)
        for h, cs, rows in zip(pair, cols, (slice(0, t_new), slice(t_new, 2 * t_new))):
            o = (lax.dot_general(pc, vt_ref[h].astype(BF16), contract_last, preferred_element_type=F32)
                 + jnp.dot(pn, vtail_ref[:, cs].astype(BF16), preferred_element_type=F32)) / s
            o_ref[:, cs] = o[rows]


def _attn_sample(q, kn, vn, cache_kt, cache_vt, bias, cnt, *, layer):
    db, t_new, width = q.shape
    wc = cache_kt.shape[-1]
    nk = wc + BLOCK
    assert HEADS_PER_TILE * t_new == 8 and bias.shape == (N_HEAD_TILES, HEADS_PER_TILE * t_new, nk)
    assert cache_kt.shape[1:] == (db, N_HEADS, HEAD_DIM, wc)
    new_blk = pl.BlockSpec((None, t_new, width), lambda i: (i, 0, 0))
    cache_blk = pl.BlockSpec((None, None, N_HEADS, HEAD_DIM, wc), lambda i: (layer, i, 0, 0, 0))
    return pl.pallas_call(
        functools.partial(_attn_sample_kernel, t_new=t_new, wc=wc),
        grid=(db,),
        in_specs=[new_blk, new_blk, new_blk,
                  _resident(bias.shape, lambda i: (0, 0, 0)), _resident(cnt.shape, lambda i: (0, 0)),
                  cache_blk, cache_blk],
        out_specs=new_blk,
        out_shape=jax.ShapeDtypeStruct((db, t_new, width), F32),
        scratch_shapes=[pltpu.VMEM((BLOCK, width), F32)] * 2,
        compiler_params=_params(("arbitrary",)),
        name="attn_sample",
    )(q, kn, vn, bias, cnt, cache_kt, cache_vt)


def _out_proj_kernel(x_ref, a_ref, p_ref, w_ref, o_ref, *, tn):
    a = a_ref[...].astype(BF16)
    p = p_ref[...]
    for c in range(o_ref.shape[1] // tn):
        cs = slice(c * tn, (c + 1) * tn)
        mixed = (jnp.dot(a, w_ref[0:ATTN_WIDTH, cs], preferred_element_type=F32)
                 + jnp.dot(p, w_ref[ATTN_WIDTH:, cs], preferred_element_type=F32))
        o_ref[:, cs] = x_ref[:, cs] + mixed


def _out_proj(x, attn, pool, w_out, *, tm):
    m, d = x.shape
    row_block = lambda width: pl.BlockSpec((tm, width), lambda i: (i, 0))
    return pl.pallas_call(
        functools.partial(_out_proj_kernel, tn=512),
        grid=(m // tm,),
        in_specs=[row_block(d), row_block(ATTN_WIDTH), row_block(POOL_WIDTH),
                  _resident(w_out.shape, lambda i: (0, 0))],
        out_specs=row_block(d),
        out_shape=jax.ShapeDtypeStruct((m, d), F32),
        compiler_params=_params(("arbitrary",)),
        name="out_proj",
    )(x, attn, pool, w_out)


def _ffn_kernel(x_ref, g_ref, wg_ref, wu_ref, wd_ref, gf_ref, o_ref, h_ref, *, final_norm):
    j = pl.program_id(1)

    @pl.when(j == 0)
    def _():
        xf = x_ref[...]
        h_ref[...] = _rmsnorm_rows(xf, g_ref[...]).astype(BF16)
        o_ref[...] = xf

    h = h_ref[...]
    gate = jnp.dot(h, wg_ref[...], preferred_element_type=F32)
    up = jnp.dot(h, wu_ref[...], preferred_element_type=F32)
    act = (gate * jax.nn.sigmoid(gate) * up).astype(BF16)
    o_ref[...] += jnp.dot(act, wd_ref[...], preferred_element_type=F32)

    if final_norm:
        @pl.when(j == pl.num_programs(1) - 1)
        def _():
            o_ref[...] = _rmsnorm_rows(o_ref[...], gf_ref[...])


def _ffn(x, g, w_gate, w_up, w_down, g_final, *, tm, tf, final_norm):
    m, d = x.shape
    f = w_gate.shape[1]
    assert m % tm == 0 and f % tf == 0
    return pl.pallas_call(
        functools.partial(_ffn_kernel, final_norm=final_norm),
        grid=(m // tm, f // tf),
        in_specs=[pl.BlockSpec((tm, d), lambda i, j: (i, 0)),
                  _resident((1, d), lambda i, j: (0, 0)),
                  pl.BlockSpec((d, tf), lambda i, j: (0, j)),
                  pl.BlockSpec((d, tf), lambda i, j: (0, j)),
                  pl.BlockSpec((tf, d), lambda i, j: (j, 0)),
                  _resident((1, d), lambda i, j: (0, 0))],
        out_specs=pl.BlockSpec((tm, d), lambda i, j: (i, 0)),
        out_shape=jax.ShapeDtypeStruct((m, d), F32),
        scratch_shapes=[pltpu.VMEM((tm, d), BF16)],
        compiler_params=_params(("arbitrary", "arbitrary")),
        name="ffn",
    )(x, g, w_gate, w_up, w_down, g_final)


def _t5_bucket(dist):
    max_exact = NUM_BUCKETS // 2
    df = jnp.maximum(dist, 1).astype(F32)
    large = max_exact + (jnp.log(df / max_exact) / math.log(MAX_WINDOW / max_exact)
                         * (NUM_BUCKETS - max_exact)).astype(jnp.int32)
    large = jnp.minimum(large, NUM_BUCKETS - 1)
    return jnp.where(dist < max_exact, dist, large)


BIAS_PERIOD = 3 * BLOCK


def _prompt_bias_vectors(rel_bias):
    vecs = []
    for window, dil in DILATED_CONFIGS:
        sub_w = window // dil
        assert sub_w <= BLOCK
        bias_sub = rel_bias[_t5_bucket(dil * jnp.arange(sub_w + 1))].T.astype(F32) * LOG2E
        vec = jnp.full((N_HEADS, BIAS_PERIOD), -jnp.inf, F32)
        vecs.append(vec.at[:, BLOCK - sub_w:BLOCK + 1].set(bias_sub[:, ::-1]))
    vecs = jnp.stack(vecs, axis=1)
    vecs = vecs.reshape(N_HEAD_TILES, HEADS_PER_TILE, len(DILATED_CONFIGS), BIAS_PERIOD)
    return vecs.transpose(0, 2, 1, 3).reshape(N_HEAD_TILES, -1, BIAS_PERIOD)


def _sample_tables(rel_bias, wc, t_new):
    nk = wc + BLOCK
    tail = np.where(np.arange(BLOCK) < t_new, wc + np.arange(BLOCK), -1)
    key_pos = np.concatenate([np.arange(wc), tail])
    dist = wc + np.arange(t_new)[:, None] - key_pos[None, :]
    listed = (key_pos >= 0)[None, :] & (dist >= 0)
    cnt = sum((listed & (dist <= w) & (dist % d == 0)).astype(np.float32) for w, d in DILATED_CONFIGS)
    max_dist = wc + t_new - 1
    by_dist = rel_bias[_t5_bucket(jnp.arange(max_dist + 1))].T.astype(F32) * LOG2E
    rev = jnp.pad(by_dist[:, ::-1], ((0, 0), (0, nk)))
    bias = jnp.stack([rev[:, t_new - 1 - t:t_new - 1 - t + nk] for t in range(t_new)], axis=1)
    bias = jnp.where(jnp.asarray(cnt > 0)[None], bias, -jnp.inf)
    bias = bias.reshape(N_HEAD_TILES, HEADS_PER_TILE * t_new, nk)
    return bias, jnp.asarray(np.tile(cnt, (HEADS_PER_TILE, 1)))


def kernel(x_prompt, x_sample, cache_k, cache_v, state_pool, rel_bias, norm_mix, w_in, w_pool,
           pool_scale, w_out, norm_ffn, w_gate, w_up, w_down, norm_final):
    b, seq, d = x_prompt.shape
    db, t_new, _ = x_sample.shape
    depth = w_in.shape[0]
    wc = cache_k.shape[2]
    assert depth >= 1 and wc == MAX_WINDOW and seq >= MAX_WINDOW and t_new <= POOL_STATE

    tm_prompt = 512
    m_sample = db * t_new
    gf = norm_final.reshape(1, d)
    prompt_bias = _prompt_bias_vectors(rel_bias)
    sample_bias, sample_cnt = _sample_tables(rel_bias, wc, t_new)
    cache_kt, cache_vt = (c.transpose(0, 1, 3, 4, 2) for c in (cache_k, cache_v))
    zero_hist = jnp.zeros((POOL_HIST, POOL_WIDTH), F32)

    def to_tok_major(a):
        return a.transpose(1, 0, 2).reshape(a.shape[1] * db, a.shape[2])

    def to_seq_major(a):
        return a.reshape(a.shape[0] // db, db, a.shape[1]).transpose(1, 0, 2)

    xp = x_prompt.reshape(b * seq, d)
    xs = to_tok_major(x_sample)
    outs = {name: [] for name in ("kp", "vp", "pp", "ks", "vs", "ps")}
    for l in range(depth):
        g_mix = norm_mix[l].reshape(1, d)
        g_ffn = norm_ffn[l].reshape(1, d)
        w_in_l = w_in[l].astype(BF16)
        w_pool_l = w_pool[l].astype(BF16)
        scale_l = pool_scale[l].reshape(1, POOL_WIDTH)
        last = l == depth - 1

        tiles_per_seq = seq // tm_prompt
        q, k, v, kt, vt, pool, utail = _in_proj(
            xp, g_mix, w_in_l, zero_hist, w_pool_l, scale_l, tm=tm_prompt, tiles_per_seq=tiles_per_seq,
            shift=1, n_hist=0, tail_rows=POOL_HIST)
        attn, w_out_l, w_gate_l, w_up_l, w_down_l = _attn_prompt(
            q.reshape(b, seq, -1), k.reshape(b, seq, -1), v.reshape(b, seq, -1), prompt_bias,
            w_out, w_gate, w_up, w_down, layer=l)
        xp = _out_proj(xp, attn.reshape(b * seq, -1), pool, w_out_l, tm=tm_prompt)
        xp = _ffn(xp, g_ffn, w_gate_l, w_up_l, w_down_l, gf, tm=2 * tm_prompt, tf=512, final_norm=last)
        win = min(MAX_WINDOW, seq)
        outs["kp"].append(kt.transpose(0, 3, 1, 2)[:, -win:])
        outs["vp"].append(vt.transpose(0, 3, 1, 2)[:, -win:])
        outs["pp"].append(utail.reshape(b, tiles_per_seq, POOL_HIST, -1)[:, -1, -POOL_STATE:])

        hist = jnp.concatenate([jnp.zeros((db, POOL_WIDTH), F32), to_tok_major(state_pool[l])], axis=0)
        q, k, v, _, _, pool, utail = _in_proj(
            xs, g_mix, w_in_l, hist, w_pool_l, scale_l, tm=m_sample, tiles_per_seq=1, shift=db,
            n_hist=POOL_STATE, tail_rows=m_sample)
        q, k, v, u = (to_seq_major(a) for a in (q, k, v, utail[0]))
        attn = _attn_sample(q, k, v, cache_kt, cache_vt, sample_bias, sample_cnt, layer=l)
        xs = _out_proj(xs, to_tok_major(attn), pool, w_out_l, tm=m_sample)
        xs = _ffn(xs, g_ffn, w_gate_l, w_up_l, w_down_l, gf, tm=m_sample, tf=512, final_norm=last)
        outs["ks"].append(k.reshape(db, t_new, N_HEADS, HEAD_DIM))
        outs["vs"].append(v.reshape(db, t_new, N_HEADS, HEAD_DIM))
        outs["ps"].append(jnp.concatenate([state_pool[l], u], axis=1)[:, -POOL_STATE:])

    y_prompt = xp.reshape(b, seq, d)
    y_sample = to_seq_major(xs)
    return (y_prompt, y_sample, jnp.stack(outs["kp"]), jnp.stack(outs["vp"]), jnp.stack(outs["pp"]),
            jnp.stack(outs["ks"]), jnp.stack(outs["vs"]), jnp.stack(outs["ps"]))
```

```python
import functools
import math

import numpy as np

import jax
import jax.numpy as jnp
from jax import lax
from jax.experimental import pallas as pl
from jax.experimental.pallas import tpu as pltpu

HEAD_DIM = 64
N_HEADS = 16
ATTN_WIDTH = N_HEADS * HEAD_DIM
DILATED_CONFIGS = ((128, 1), (512, 4), (2048, 16))
MAX_WINDOW = 2048
BLOCK = 128
POOL_WINDOWS = (2, 4, 8, 16)
POOL_GROUP_DIM = 256
POOL_WIDTH = len(POOL_WINDOWS) * POOL_GROUP_DIM
POOL_STATE = max(POOL_WINDOWS) - 1
POOL_HIST = max(POOL_WINDOWS)
NUM_BUCKETS = 32
EPS = 1e-6
SCALE = HEAD_DIM ** -0.5
LOG2E = math.log2(math.e)

LANES = 128
HEADS_PER_TILE = LANES // HEAD_DIM
N_HEAD_TILES = N_HEADS // HEADS_PER_TILE
VMEM_LIMIT = 56 * 1024 * 1024

F32 = jnp.float32
BF16 = jnp.bfloat16


def _params(semantics):
    return pltpu.CompilerParams(dimension_semantics=semantics, vmem_limit_bytes=VMEM_LIMIT)


def _resident(shape, index_map):
    return pl.BlockSpec(shape, index_map, pipeline_mode=pl.Buffered(1))


def _rmsnorm_rows(xf, g):
    return xf * lax.rsqrt(jnp.mean(xf * xf, axis=-1, keepdims=True) + EPS) * g


def _in_proj_kernel(x_ref, g_ref, w_ref, hist_ref, wp_ref, ps_ref,
                    q_ref, k_ref, v_ref, kt_ref, vt_ref, pool_ref, utail_ref,
                    h_ref, ucat_ref, pooled_ref, *, tm, tn, tiles_per_seq, shift, n_hist):
    seq_tile = pl.program_id(0) % tiles_per_seq
    hist_rows = POOL_HIST * shift
    tail_rows = utail_ref.shape[0]

    @pl.when(seq_tile == 0)
    def _():
        ucat_ref[0:hist_rows, :] = hist_ref[...]

    h_ref[...] = _rmsnorm_rows(x_ref[...], g_ref[...]).astype(BF16)
    per_out = ATTN_WIDTH // tn
    heads_per_chunk = tn // HEAD_DIM
    for c in list(range(3 * per_out, 4 * per_out)) + list(range(3 * per_out)):
        res = jnp.dot(h_ref[...], w_ref[:, c * tn:(c + 1) * tn], preferred_element_type=F32)
        which, part = divmod(c, per_out)
        cs = slice(part * tn, (part + 1) * tn)
        if which == 0:
            q_ref[:, cs] = res * (SCALE * LOG2E)
        elif which == 3:
            ucat_ref[hist_rows:hist_rows + tm, cs] = res
        else:
            nat, tr = ((k_ref, kt_ref), (v_ref, vt_ref))[which - 1]
            nat[:, cs] = res
            heads = slice(part * heads_per_chunk, (part + 1) * heads_per_chunk)
            tr[heads] = res.T.reshape(heads_per_chunk, HEAD_DIM, tm)

    utail_ref[...] = ucat_ref[hist_rows + tm - tail_rows:hist_rows + tm, :]

    chunk = min(128, tm)
    for r0 in range(0, tm, chunk):
        row = lax.broadcasted_iota(jnp.int32, (chunk, 1), 0)
        tok = (seq_tile * tm + r0 + row) // shift
        for g, w in enumerate(POOL_WINDOWS):
            cs = slice(g * POOL_GROUP_DIM, (g + 1) * POOL_GROUP_DIM)
            u0 = ucat_ref[hist_rows + r0:hist_rows + r0 + chunk, cs]
            acc = u0
            for j in range(1, w):
                lo = hist_rows + r0 - j * shift
                acc = acc + ucat_ref[lo:lo + chunk, cs]
            cnt = jnp.minimum(tok + (n_hist + 1), w).astype(F32)
            pooled_ref[r0:r0 + chunk, cs] = (acc / cnt - u0).astype(BF16)

    for g in range(len(POOL_WINDOWS)):
        cs = slice(g * POOL_GROUP_DIM, (g + 1) * POOL_GROUP_DIM)
        y = jnp.dot(pooled_ref[:, cs], wp_ref[g], preferred_element_type=F32) * ps_ref[:, cs]
        pool_ref[:, cs] = y.astype(BF16)

    if tiles_per_seq > 1:
        ucat_ref[0:hist_rows, :] = ucat_ref[tm:tm + hist_rows, :]


def _in_proj(x, g, w_in, hist, w_pool, pool_scale, *, tm, tiles_per_seq, shift, n_hist, tail_rows):
    m, d = x.shape
    in_cols = w_in.shape[1]
    hist_rows = POOL_HIST * shift
    n_tiles = m // tm
    assert m % tm == 0 and n_tiles % tiles_per_seq == 0 and hist.shape == (hist_rows, POOL_WIDTH)
    assert tiles_per_seq == 1 or tm >= hist_rows
    row_block = lambda width: pl.BlockSpec((tm, width), lambda i: (i, 0))
    tr_block = pl.BlockSpec((None, N_HEADS, HEAD_DIM, tm),
                            lambda i: (i // tiles_per_seq, 0, 0, i % tiles_per_seq))
    tr_shape = jax.ShapeDtypeStruct((n_tiles // tiles_per_seq, N_HEADS, HEAD_DIM, tiles_per_seq * tm), F32)
    kern = functools.partial(_in_proj_kernel, tm=tm, tn=512, tiles_per_seq=tiles_per_seq,
                             shift=shift, n_hist=n_hist)
    return pl.pallas_call(
        kern,
        grid=(n_tiles,),
        in_specs=[
            row_block(d),
            _resident((1, d), lambda i: (0, 0)),
            _resident((d, in_cols), lambda i: (0, 0)),
            _resident((hist_rows, POOL_WIDTH), lambda i: (0, 0)),
            _resident(w_pool.shape, lambda i: (0, 0, 0)),
            _resident((1, POOL_WIDTH), lambda i: (0, 0)),
        ],
        out_specs=[row_block(ATTN_WIDTH)] * 3 + [tr_block] * 2
        + [row_block(POOL_WIDTH), pl.BlockSpec((None, tail_rows, POOL_WIDTH), lambda i: (i, 0, 0))],
        out_shape=[jax.ShapeDtypeStruct((m, ATTN_WIDTH), F32)] * 3 + [tr_shape] * 2
        + [jax.ShapeDtypeStruct((m, POOL_WIDTH), BF16),
           jax.ShapeDtypeStruct((n_tiles, tail_rows, POOL_WIDTH), F32)],
        scratch_shapes=[
            pltpu.VMEM((tm, d), BF16),
            pltpu.VMEM((hist_rows + tm, POOL_WIDTH), F32),
            pltpu.VMEM((tm, POOL_WIDTH), BF16),
        ],
        compiler_params=_params(("arbitrary",)),
        name="in_proj",
    )(x, g, w_in, hist, w_pool, pool_scale)


def _attn_prompt_kernel(q_ref, k_ref, v_ref, bvec_ref, wo_ref, wg_ref, wu_ref, wd_ref,
                        o_ref, wo_bf_ref, wg_bf_ref, wu_bf_ref, wd_bf_ref, bias_ref,
                        q4_ref, k4_ref, v4_ref, m4_ref, l4_ref, acc4_ref, m_ref, l_ref, acc_ref, *, seq):
    step = pl.program_id(0) * pl.num_programs(1) + pl.program_id(1)
    for src, dst in ((wo_ref, wo_bf_ref), (wg_ref, wg_bf_ref), (wu_ref, wu_bf_ref)):
        dst[...] = src[...].astype(BF16)

    @pl.when(step % 2 == 0)
    def _():
        wd_bf_ref[...] = wd_ref[...].astype(BF16)

    @pl.when(pl.program_id(1) == 0)
    def _():
        for c in range(len(DILATED_CONFIGS)):
            for hh in range(HEADS_PER_TILE):
                vec = bvec_ref[c * HEADS_PER_TILE + hh:c * HEADS_PER_TILE + hh + 1, :]
                rows = jnp.broadcast_to(vec, (BLOCK, BIAS_PERIOD))
                bias_ref[c, hh] = pltpu.roll(rows, 0, 1, stride=1, stride_axis=0)[:, :2 * BLOCK]

    is_a = lax.broadcasted_iota(jnp.int32, (BLOCK, LANES), 1) < HEAD_DIM
    quarter = seq // 4
    nat_cfg, r4_cfg, r16_cfg = range(3)
    assert DILATED_CONFIGS[r4_cfg][1] == 4 and DILATED_CONFIGS[r16_cfg][1] == 16

    def for_chunks(fn):
        for r in range(4):
            for j in range(quarter // BLOCK):
                fn(pl.ds(r + 4 * BLOCK * j, BLOCK, stride=4), pl.ds(r * quarter + BLOCK * j, BLOCK))

    def to_residue_major(nat, r4):
        for src, dst in zip(nat, r4):
            def move(nat_rows, r4_rows, src=src, dst=dst):
                dst[r4_rows, :] = src[nat_rows, :]
            for_chunks(move)

    def to_natural(r4, nat):
        for src, dst in zip(r4, nat):
            def move(nat_rows, r4_rows, src=src, dst=dst):
                dst[nat_rows, :] = src[r4_rows, :]
            for_chunks(move)

    def compute(c, srcs, qr, kr, has_prev):
        q_src, k_src, v_src = srcs
        qb = q_src[qr, :]
        kb = k_src[kr, :].astype(BF16)
        vb = v_src[kr, :].astype(BF16)
        zero = jnp.zeros_like(qb)
        qs = jnp.concatenate([jnp.where(is_a, qb, zero), jnp.where(is_a, zero, qb)], axis=0).astype(BF16)
        bias = bias_ref[c] if has_prev else bias_ref[c, :, :, BLOCK:]
        lg = lax.dot_general(qs, kb, (((1,), (1,)), ((), ())), preferred_element_type=F32)
        lg = lg + bias.reshape(HEADS_PER_TILE * BLOCK, bias.shape[-1])
        m = jnp.max(lg, axis=-1, keepdims=True)
        p = jnp.exp2(lg - m)
        s = jnp.sum(p, axis=-1, keepdims=True)
        o = jnp.dot(p.astype(BF16), vb, preferred_element_type=F32)
        return tuple(jnp.where(is_a, x[:BLOCK], x[BLOCK:]) for x in (m, s, o))

    def rescale(m_old, m_new):
        e = jnp.exp2(-jnp.abs(m_old - m_new))
        keep = m_old >= m_new
        return jnp.where(keep, 1.0, e), jnp.where(keep, e, 1.0)

    def first_visit(state, qr, m_new, s_new, o_new):
        for ref, val in zip(state, (m_new, s_new, o_new)):
            ref[qr, :] = val

    def merge(state, qr, m_new, s_new, o_new):
        m_st, l_st, acc_st = state
        m_old = m_st[qr, :]
        a, b = rescale(m_old, m_new)
        m_st[qr, :] = jnp.maximum(m_old, m_new)
        l_st[qr, :] = l_st[qr, :] * a + s_new * b
        acc_st[qr, :] = acc_st[qr, :] * a + o_new * b

    def last_visit(state, qr, m_new, s_new, o_new):
        m_st, l_st, acc_st = state
        a, b = rescale(m_st[qr, :], m_new)
        o_ref[qr, :] = ((acc_st[qr, :] * a + o_new * b) / (l_st[qr, :] * a + s_new * b)).astype(o_ref.dtype)

    def run(c, srcs, units, visit, state):
        for qr, kr, has_prev in units:
            visit(state, qr, *compute(c, srcs, qr, kr, has_prev))

    def block_units(base, n_blocks):
        return [(pl.ds(base + n * BLOCK, BLOCK),
                 pl.ds(base + (n - 1) * BLOCK, 2 * BLOCK) if n else pl.ds(base, BLOCK), n > 0)
                for n in range(n_blocks)]

    nat_src, r4_src = (q_ref, k_ref, v_ref), (q4_ref, k4_ref, v4_ref)
    nat_state, r4_state = (m_ref, l_ref, acc_ref), (m4_ref, l4_ref, acc4_ref)
    to_residue_major(nat_src, r4_src)

    units = [u for r in range(4) for u in block_units(r * quarter, quarter // BLOCK)]
    run(r4_cfg, r4_src, units, first_visit, r4_state)

    assert seq == 16 * BLOCK
    units = []
    for r16 in range(16):
        rows16 = pl.ds((r16 % 4) * quarter + r16 // 4, BLOCK, stride=4)
        units.append((rows16, rows16, False))
    run(r16_cfg, r4_src, units, merge, r4_state)

    to_natural(r4_state, nat_state)
    run(nat_cfg, nat_src, block_units(0, seq // BLOCK), last_visit, nat_state)


def _attn_prompt(q, k, v, bias_vecs, w_out, w_gate, w_up, w_down, *, layer):
    b, seq, _ = q.shape
    assert all(seq % (BLOCK * dil) == 0 for _, dil in DILATED_CONFIGS)
    steps = N_HEAD_TILES * b
    bf16_rows = 16
    blk = pl.BlockSpec((None, seq, LANES), lambda t, i: (i, 0, t))

    def slab(w, every):
        rows = w.shape[1] * every // steps
        assert w.shape[1] * every % steps == 0 and rows % bf16_rows == 0
        return pl.BlockSpec((None, rows, w.shape[2]), lambda t, i: (layer, (t * b + i) // every, 0))

    def slab_out(w, every):
        rows = w.shape[1] * every // steps
        return pl.BlockSpec((rows, w.shape[2]), lambda t, i: ((t * b + i) // every, 0))

    weights = ((w_out, 1), (w_gate, 1), (w_up, 1), (w_down, 2))
    return pl.pallas_call(
        functools.partial(_attn_prompt_kernel, seq=seq),
        grid=(N_HEAD_TILES, b),
        in_specs=[blk, blk, blk,
                  pl.BlockSpec((None,) + bias_vecs.shape[1:], lambda t, i: (t, 0, 0))]
        + [slab(w, every) for w, every in weights],
        out_specs=[blk] + [slab_out(w, every) for w, every in weights],
        out_shape=[jax.ShapeDtypeStruct((b, seq, ATTN_WIDTH), BF16)]
        + [jax.ShapeDtypeStruct(w.shape[1:], BF16) for w, _ in weights],
        scratch_shapes=[pltpu.VMEM((len(DILATED_CONFIGS), HEADS_PER_TILE, BLOCK, 2 * BLOCK), F32)]
        + [pltpu.VMEM((seq, LANES), F32)] * 9,
        compiler_params=_params(("arbitrary", "arbitrary")),
        name="attn_prompt",
    )(q, k, v, bias_vecs, *(w for w, _ in weights))


def _attn_sample_kernel(q_ref, kn_ref, vn_ref, bias_ref, cnt_ref, kt_ref, vt_ref, o_ref,
                        ktail_ref, vtail_ref, *, t_new, wc):
    @pl.when(pl.program_id(0) == 0)
    def _():
        ktail_ref[...] = jnp.zeros_like(ktail_ref)
        vtail_ref[...] = jnp.zeros_like(vtail_ref)

    ktail_ref[0:t_new, :] = kn_ref[...]
    vtail_ref[0:t_new, :] = vn_ref[...]
    q = q_ref[...]
    first = lax.broadcasted_iota(jnp.int32, (HEADS_PER_TILE * t_new, 1), 0) < t_new
    contract_last = (((1,), (1,)), ((), ()))
    cnt_cache, cnt_new = cnt_ref[:, :wc], cnt_ref[:, wc:]
    for j in range(N_HEAD_TILES):
        pair = (HEADS_PER_TILE * j, HEADS_PER_TILE * j + 1)
        cols = [slice(h * HEAD_DIM, (h + 1) * HEAD_DIM) for h in pair]
        lhs = jnp.concatenate([q[:, cs] for cs in cols], axis=0).astype(BF16)
        own = lambda a, b: jnp.where(first, a, b)
        lc = own(*(jnp.dot(lhs, kt_ref[h].astype(BF16), preferred_element_type=F32) for h in pair))
        ln = own(*(lax.dot_general(lhs, ktail_ref[:, cs].astype(BF16), contract_last,
                                   preferred_element_type=F32) for cs in cols))
        lc = lc + bias_ref[j, :, :wc]
        ln = ln + bias_ref[j, :, wc:]
        m = jnp.maximum(jnp.max(lc, axis=-1, keepdims=True), jnp.max(ln, axis=-1, keepdims=True))
        pc = cnt_cache * jnp.exp2(lc - m)
        pn = cnt_new * jnp.exp2(ln - m)
        s = jnp.sum(pc, axis=-1, keepdims=True) + jnp.sum(pn, axis=-1, keepdims=True)
        pc, pn = pc.astype(BF16), pn.astype(BF16)
        for h, cs, rows in zip(pair, cols, (slice(0, t_new), slice(t_new, 2 * t_new))):
            o = (lax.dot_general(pc, vt_ref[h].astype(BF16), contract_last, preferred_element_type=F32)
                 + jnp.dot(pn, vtail_ref[:, cs].astype(BF16), preferred_element_type=F32)) / s
            o_ref[:, cs] = o[rows]


def _attn_sample(q, kn, vn, cache_kt, cache_vt, bias, cnt, *, layer):
    db, t_new, width = q.shape
    wc = cache_kt.shape[-1]
    nk = wc + BLOCK
    assert HEADS_PER_TILE * t_new == 8 and bias.shape == (N_HEAD_TILES, HEADS_PER_TILE * t_new, nk)
    assert cache_kt.shape[1:] == (db, N_HEADS, HEAD_DIM, wc)
    new_blk = pl.BlockSpec((None, t_new, width), lambda i: (i, 0, 0))
    cache_blk = pl.BlockSpec((None, None, N_HEADS, HEAD_DIM, wc), lambda i: (layer, i, 0, 0, 0))
    return pl.pallas_call(
        functools.partial(_attn_sample_kernel, t_new=t_new, wc=wc),
        grid=(db,),
        in_specs=[new_blk, new_blk, new_blk,
                  _resident(bias.shape, lambda i: (0, 0, 0)), _resident(cnt.shape, lambda i: (0, 0)),
                  cache_blk, cache_blk],
        out_specs=new_blk,
        out_shape=jax.ShapeDtypeStruct((db, t_new, width), F32),
        scratch_shapes=[pltpu.VMEM((BLOCK, width), F32)] * 2,
        compiler_params=_params(("arbitrary",)),
        name="attn_sample",
    )(q, kn, vn, bias, cnt, cache_kt, cache_vt)


def _out_proj_kernel(x_ref, a_ref, p_ref, w_ref, o_ref, *, tn):
    a = a_ref[...].astype(BF16)
    p = p_ref[...]
    for c in range(o_ref.shape[1] // tn):
        cs = slice(c * tn, (c + 1) * tn)
        mixed = (jnp.dot(a, w_ref[0:ATTN_WIDTH, cs], preferred_element_type=F32)
                 + jnp.dot(p, w_ref[ATTN_WIDTH:, cs], preferred_element_type=F32))
        o_ref[:, cs] = x_ref[:, cs] + mixed


def _out_proj(x, attn, pool, w_out, *, tm):
    m, d = x.shape
    row_block = lambda width: pl.BlockSpec((tm, width), lambda i: (i, 0))
    return pl.pallas_call(
        functools.partial(_out_proj_kernel, tn=512),
        grid=(m // tm,),
        in_specs=[row_block(d), row_block(ATTN_WIDTH), row_block(POOL_WIDTH),
                  _resident(w_out.shape, lambda i: (0, 0))],
        out_specs=row_block(d),
        out_shape=jax.ShapeDtypeStruct((m, d), F32),
        compiler_params=_params(("arbitrary",)),
        name="out_proj",
    )(x, attn, pool, w_out)


def _ffn_kernel(x_ref, g_ref, wg_ref, wu_ref, wd_ref, gf_ref, o_ref, h_ref, *, final_norm):
    j = pl.program_id(1)

    @pl.when(j == 0)
    def _():
        xf = x_ref[...]
        h_ref[...] = _rmsnorm_rows(xf, g_ref[...]).astype(BF16)
        o_ref[...] = xf

    h = h_ref[...]
    gate = jnp.dot(h, wg_ref[...], preferred_element_type=F32)
    up = jnp.dot(h, wu_ref[...], preferred_element_type=F32)
    act = (gate * jax.nn.sigmoid(gate) * up).astype(BF16)
    o_ref[...] += jnp.dot(act, wd_ref[...], preferred_element_type=F32)

    if final_norm:
        @pl.when(j == pl.num_programs(1) - 1)
        def _():
            o_ref[...] = _rmsnorm_rows(o_ref[...], gf_ref[...])


def _ffn(x, g, w_gate, w_up, w_down, g_final, *, tm, tf, final_norm):
    m, d = x.shape
    f = w_gate.shape[1]
    assert m % tm == 0 and f % tf == 0
    return pl.pallas_call(
        functools.partial(_ffn_kernel, final_norm=final_norm),
        grid=(m // tm, f // tf),
        in_specs=[pl.BlockSpec((tm, d), lambda i, j: (i, 0)),
                  _resident((1, d), lambda i, j: (0, 0)),
                  pl.BlockSpec((d, tf), lambda i, j: (0, j)),
                  pl.BlockSpec((d, tf), lambda i, j: (0, j)),
                  pl.BlockSpec((tf, d), lambda i, j: (j, 0)),
                  _resident((1, d), lambda i, j: (0, 0))],
        out_specs=pl.BlockSpec((tm, d), lambda i, j: (i, 0)),
        out_shape=jax.ShapeDtypeStruct((m, d), F32),
        scratch_shapes=[pltpu.VMEM((tm, d), BF16)],
        compiler_params=_params(("arbitrary", "arbitrary")),
        name="ffn",
    )(x, g, w_gate, w_up, w_down, g_final)


def _t5_bucket(dist):
    max_exact = NUM_BUCKETS // 2
    df = jnp.maximum(dist, 1).astype(F32)
    large = max_exact + (jnp.log(df / max_exact) / math.log(MAX_WINDOW / max_exact)
                         * (NUM_BUCKETS - max_exact)).astype(jnp.int32)
    large = jnp.minimum(large, NUM_BUCKETS - 1)
    return jnp.where(dist < max_exact, dist, large)


BIAS_PERIOD = 3 * BLOCK


def _prompt_bias_vectors(rel_bias):
    vecs = []
    for window, dil in DILATED_CONFIGS:
        sub_w = window // dil
        assert sub_w <= BLOCK
        bias_sub = rel_bias[_t5_bucket(dil * jnp.arange(sub_w + 1))].T.astype(F32) * LOG2E
        vec = jnp.full((N_HEADS, BIAS_PERIOD), -jnp.inf, F32)
        vecs.append(vec.at[:, BLOCK - sub_w:BLOCK + 1].set(bias_sub[:, ::-1]))
    vecs = jnp.stack(vecs, axis=1)
    vecs = vecs.reshape(N_HEAD_TILES, HEADS_PER_TILE, len(DILATED_CONFIGS), BIAS_PERIOD)
    return vecs.transpose(0, 2, 1, 3).reshape(N_HEAD_TILES, -1, BIAS_PERIOD)


def _sample_tables(rel_bias, wc, t_new):
    nk = wc + BLOCK
    tail = np.where(np.arange(BLOCK) < t_new, wc + np.arange(BLOCK), -1)
    key_pos = np.concatenate([np.arange(wc), tail])
    dist = wc + np.arange(t_new)[:, None] - key_pos[None, :]
    listed = (key_pos >= 0)[None, :] & (dist >= 0)
    cnt = sum((listed & (dist <= w) & (dist % d == 0)).astype(np.float32) for w, d in DILATED_CONFIGS)
    max_dist = wc + t_new - 1
    by_dist = rel_bias[_t5_bucket(jnp.arange(max_dist + 1))].T.astype(F32) * LOG2E
    rev = jnp.pad(by_dist[:, ::-1], ((0, 0), (0, nk)))
    bias = jnp.stack([rev[:, t_new - 1 - t:t_new - 1 - t + nk] for t in range(t_new)], axis=1)
    bias = jnp.where(jnp.asarray(cnt > 0)[None], bias, -jnp.inf)
    bias = bias.reshape(N_HEAD_TILES, HEADS_PER_TILE * t_new, nk)
    return bias, jnp.asarray(np.tile(cnt, (HEADS_PER_TILE, 1)))


def kernel(x_prompt, x_sample, cache_k, cache_v, state_pool, rel_bias, norm_mix, w_in, w_pool,
           pool_scale, w_out, norm_ffn, w_gate, w_up, w_down, norm_final):
    b, seq, d = x_prompt.shape
    db, t_new, _ = x_sample.shape
    depth = w_in.shape[0]
    wc = cache_k.shape[2]
    assert depth >= 1 and wc == MAX_WINDOW and seq >= MAX_WINDOW and t_new <= POOL_STATE

    tm_prompt = 512
    m_sample = db * t_new
    gf = norm_final.reshape(1, d)
    prompt_bias = _prompt_bias_vectors(rel_bias)
    sample_bias, sample_cnt = _sample_tables(rel_bias, wc, t_new)
    cache_kt, cache_vt = (c.transpose(0, 1, 3, 4, 2) for c in (cache_k, cache_v))
    zero_hist = jnp.zeros((POOL_HIST, POOL_WIDTH), F32)

    def to_tok_major(a):
        return a.transpose(1, 0, 2).reshape(a.shape[1] * db, a.shape[2])

    def to_seq_major(a):
        return a.reshape(a.shape[0] // db, db, a.shape[1]).transpose(1, 0, 2)

    xp = x_prompt.reshape(b * seq, d)
    xs = to_tok_major(x_sample)
    outs = {name: [] for name in ("kp", "vp", "pp", "ks", "vs", "ps")}
    for l in range(depth):
        g_mix = norm_mix[l].reshape(1, d)
        g_ffn = norm_ffn[l].reshape(1, d)
        w_in_l = w_in[l].astype(BF16)
        w_pool_l = w_pool[l].astype(BF16)
        scale_l = pool_scale[l].reshape(1, POOL_WIDTH)
        last = l == depth - 1

        tiles_per_seq = seq // tm_prompt
        q, k, v, kt, vt, pool, utail = _in_proj(
            xp, g_mix, w_in_l, zero_hist, w_pool_l, scale_l, tm=tm_prompt, tiles_per_seq=tiles_per_seq,
            shift=1, n_hist=0, tail_rows=POOL_HIST)
        attn, w_out_l, w_gate_l, w_up_l, w_down_l = _attn_prompt(
            q.reshape(b, seq, -1), k.reshape(b, seq, -1), v.reshape(b, seq, -1), prompt_bias,
            w_out, w_gate, w_up, w_down, layer=l)
        xp = _out_proj(xp, attn.reshape(b * seq, -1), pool, w_out_l, tm=2 * tm_prompt)
        xp = _ffn(xp, g_ffn, w_gate_l, w_up_l, w_down_l, gf, tm=2 * tm_prompt, tf=512, final_norm=last)
        win = min(MAX_WINDOW, seq)
        outs["kp"].append(kt.transpose(0, 3, 1, 2)[:, -win:])
        outs["vp"].append(vt.transpose(0, 3, 1, 2)[:, -win:])
        outs["pp"].append(utail.reshape(b, tiles_per_seq, POOL_HIST, -1)[:, -1, -POOL_STATE:])

        hist = jnp.concatenate([jnp.zeros((db, POOL_WIDTH), F32), to_tok_major(state_pool[l])], axis=0)
        q, k, v, _, _, pool, utail = _in_proj(
            xs, g_mix, w_in_l, hist, w_pool_l, scale_l, tm=m_sample, tiles_per_seq=1, shift=db,
            n_hist=POOL_STATE, tail_rows=m_sample)
        q, k, v, u = (to_seq_major(a) for a in (q, k, v, utail[0]))
        attn = _attn_sample(q, k, v, cache_kt, cache_vt, sample_bias, sample_cnt, layer=l)
        xs = _out_proj(xs, to_tok_major(attn), pool, w_out_l, tm=m_sample)
        xs = _ffn(xs, g_ffn, w_gate_l, w_up_l, w_down_l, gf, tm=m_sample, tf=512, final_norm=last)
        outs["ks"].append(k.reshape(db, t_new, N_HEADS, HEAD_DIM))
        outs["vs"].append(v.reshape(db, t_new, N_HEADS, HEAD_DIM))
        outs["ps"].append(jnp.concatenate([state_pool[l], u], axis=1)[:, -POOL_STATE:])

    y_prompt = xp.reshape(b, seq, d)
    y_sample = to_seq_major(xs)
    return (y_prompt, y_sample, jnp.stack(outs["kp"]), jnp.stack(outs["vp"]), jnp.stack(outs["pp"]),
            jnp.stack(outs["ks"]), jnp.stack(outs["vs"]), jnp.stack(outs["ps"]))
```

```python
import functools
import math

import numpy as np

import jax
import jax.numpy as jnp
from jax import lax
from jax.experimental import pallas as pl
from jax.experimental.pallas import tpu as pltpu

HEAD_DIM = 64
N_HEADS = 16
ATTN_WIDTH = N_HEADS * HEAD_DIM
DILATED_CONFIGS = ((128, 1), (512, 4), (2048, 16))
MAX_WINDOW = 2048
BLOCK = 128
POOL_WINDOWS = (2, 4, 8, 16)
POOL_GROUP_DIM = 256
POOL_WIDTH = len(POOL_WINDOWS) * POOL_GROUP_DIM
POOL_STATE = max(POOL_WINDOWS) - 1
POOL_HIST = max(POOL_WINDOWS)
NUM_BUCKETS = 32
EPS = 1e-6
SCALE = HEAD_DIM ** -0.5
LOG2E = math.log2(math.e)

LANES = 128
HEADS_PER_TILE = LANES // HEAD_DIM
N_HEAD_TILES = N_HEADS // HEADS_PER_TILE
VMEM_LIMIT = 56 * 1024 * 1024

F32 = jnp.float32
BF16 = jnp.bfloat16


def _params(semantics):
    return pltpu.CompilerParams(dimension_semantics=semantics, vmem_limit_bytes=VMEM_LIMIT)


def _resident(shape, index_map):
    return pl.BlockSpec(shape, index_map, pipeline_mode=pl.Buffered(1))


def _rmsnorm_rows(xf, g):
    return xf * lax.rsqrt(jnp.mean(xf * xf, axis=-1, keepdims=True) + EPS) * g


def _in_proj_kernel(x_ref, g_ref, w_ref, hist_ref, wp_ref, ps_ref,
                    q_ref, k_ref, v_ref, kt_ref, vt_ref, pool_ref, utail_ref,
                    h_ref, ucat_ref, pooled_ref, *, tm, tn, tiles_per_seq, shift, n_hist):
    seq_tile = pl.program_id(0) % tiles_per_seq
    hist_rows = POOL_HIST * shift
    tail_rows = utail_ref.shape[0]

    @pl.when(seq_tile == 0)
    def _():
        ucat_ref[0:hist_rows, :] = hist_ref[...]

    h_ref[...] = _rmsnorm_rows(x_ref[...], g_ref[...]).astype(BF16)
    per_out = ATTN_WIDTH // tn
    heads_per_chunk = tn // HEAD_DIM
    for c in list(range(3 * per_out, 4 * per_out)) + list(range(3 * per_out)):
        res = jnp.dot(h_ref[...], w_ref[:, c * tn:(c + 1) * tn], preferred_element_type=F32)
        which, part = divmod(c, per_out)
        cs = slice(part * tn, (part + 1) * tn)
        if which == 0:
            q_ref[:, cs] = res * (SCALE * LOG2E)
        elif which == 3:
            ucat_ref[hist_rows:hist_rows + tm, cs] = res
        else:
            nat, tr = ((k_ref, kt_ref), (v_ref, vt_ref))[which - 1]
            nat[:, cs] = res
            heads = slice(part * heads_per_chunk, (part + 1) * heads_per_chunk)
            tr[heads] = res.T.reshape(heads_per_chunk, HEAD_DIM, tm)

    utail_ref[...] = ucat_ref[hist_rows + tm - tail_rows:hist_rows + tm, :]

    chunk = min(128, tm)
    for r0 in range(0, tm, chunk):
        row = lax.broadcasted_iota(jnp.int32, (chunk, 1), 0)
        tok = (seq_tile * tm + r0 + row) // shift
        for g, w in enumerate(POOL_WINDOWS):
            cs = slice(g * POOL_GROUP_DIM, (g + 1) * POOL_GROUP_DIM)
            u0 = ucat_ref[hist_rows + r0:hist_rows + r0 + chunk, cs]
            acc = u0
            for j in range(1, w):
                lo = hist_rows + r0 - j * shift
                acc = acc + ucat_ref[lo:lo + chunk, cs]
            cnt = jnp.minimum(tok + (n_hist + 1), w).astype(F32)
            pooled_ref[r0:r0 + chunk, cs] = (acc / cnt - u0).astype(BF16)

    for g in range(len(POOL_WINDOWS)):
        cs = slice(g * POOL_GROUP_DIM, (g + 1) * POOL_GROUP_DIM)
        y = jnp.dot(pooled_ref[:, cs], wp_ref[g], preferred_element_type=F32) * ps_ref[:, cs]
        pool_ref[:, cs] = y.astype(BF16)

    if tiles_per_seq > 1:
        ucat_ref[0:hist_rows, :] = ucat_ref[tm:tm + hist_rows, :]


def _in_proj(x, g, w_in, hist, w_pool, pool_scale, *, tm, tiles_per_seq, shift, n_hist, tail_rows):
    m, d = x.shape
    in_cols = w_in.shape[1]
    hist_rows = POOL_HIST * shift
    n_tiles = m // tm
    assert m % tm == 0 and n_tiles % tiles_per_seq == 0 and hist.shape == (hist_rows, POOL_WIDTH)
    assert tiles_per_seq == 1 or tm >= hist_rows
    row_block = lambda width: pl.BlockSpec((tm, width), lambda i: (i, 0))
    tr_block = pl.BlockSpec((None, N_HEADS, HEAD_DIM, tm),
                            lambda i: (i // tiles_per_seq, 0, 0, i % tiles_per_seq))
    tr_shape = jax.ShapeDtypeStruct((n_tiles // tiles_per_seq, N_HEADS, HEAD_DIM, tiles_per_seq * tm), F32)
    kern = functools.partial(_in_proj_kernel, tm=tm, tn=512, tiles_per_seq=tiles_per_seq,
                             shift=shift, n_hist=n_hist)
    return pl.pallas_call(
        kern,
        grid=(n_tiles,),
        in_specs=[
            row_block(d),
            _resident((1, d), lambda i: (0, 0)),
            _resident((d, in_cols), lambda i: (0, 0)),
            _resident((hist_rows, POOL_WIDTH), lambda i: (0, 0)),
            _resident(w_pool.shape, lambda i: (0, 0, 0)),
            _resident((1, POOL_WIDTH), lambda i: (0, 0)),
        ],
        out_specs=[row_block(ATTN_WIDTH)] * 3 + [tr_block] * 2
        + [row_block(POOL_WIDTH), pl.BlockSpec((None, tail_rows, POOL_WIDTH), lambda i: (i, 0, 0))],
        out_shape=[jax.ShapeDtypeStruct((m, ATTN_WIDTH), F32)] * 3 + [tr_shape] * 2
        + [jax.ShapeDtypeStruct((m, POOL_WIDTH), BF16),
           jax.ShapeDtypeStruct((n_tiles, tail_rows, POOL_WIDTH), F32)],
        scratch_shapes=[
            pltpu.VMEM((tm, d), BF16),
            pltpu.VMEM((hist_rows + tm, POOL_WIDTH), F32),
            pltpu.VMEM((tm, POOL_WIDTH), BF16),
        ],
        compiler_params=_params(("arbitrary",)),
        name="in_proj",
    )(x, g, w_in, hist, w_pool, pool_scale)


def _attn_prompt_kernel(q_ref, k_ref, v_ref, bvec_ref, wo_ref, wg_ref, wu_ref, wd_ref,
                        o_ref, wo_bf_ref, wg_bf_ref, wu_bf_ref, wd_bf_ref, bias_ref,
                        q4_ref, k4_ref, v4_ref, m4_ref, l4_ref, acc4_ref, m_ref, l_ref, acc_ref, *, seq):
    step = pl.program_id(0) * pl.num_programs(1) + pl.program_id(1)
    for src, dst in ((wo_ref, wo_bf_ref), (wg_ref, wg_bf_ref), (wu_ref, wu_bf_ref)):
        dst[...] = src[...].astype(BF16)

    @pl.when(step % 2 == 0)
    def _():
        wd_bf_ref[...] = wd_ref[...].astype(BF16)

    @pl.when(pl.program_id(1) == 0)
    def _():
        for c in range(len(DILATED_CONFIGS)):
            for hh in range(HEADS_PER_TILE):
                vec = bvec_ref[c * HEADS_PER_TILE + hh:c * HEADS_PER_TILE + hh + 1, :]
                rows = jnp.broadcast_to(vec, (BLOCK, BIAS_PERIOD))
                bias_ref[c, hh] = pltpu.roll(rows, 0, 1, stride=1, stride_axis=0)[:, :2 * BLOCK]

    is_a = lax.broadcasted_iota(jnp.int32, (BLOCK, LANES), 1) < HEAD_DIM
    quarter = seq // 4
    nat_cfg, r4_cfg, r16_cfg = range(3)
    assert DILATED_CONFIGS[r4_cfg][1] == 4 and DILATED_CONFIGS[r16_cfg][1] == 16

    def for_chunks(fn):
        for r in range(4):
            for j in range(quarter // BLOCK):
                fn(pl.ds(r + 4 * BLOCK * j, BLOCK, stride=4), pl.ds(r * quarter + BLOCK * j, BLOCK))

    def to_residue_major(nat, r4):
        for src, dst in zip(nat, r4):
            def move(nat_rows, r4_rows, src=src, dst=dst):
                dst[r4_rows, :] = src[nat_rows, :]
            for_chunks(move)

    def to_natural(r4, nat):
        for src, dst in zip(r4, nat):
            def move(nat_rows, r4_rows, src=src, dst=dst):
                dst[nat_rows, :] = src[r4_rows, :]
            for_chunks(move)

    def compute(c, srcs, qr, kr, has_prev):
        q_src, k_src, v_src = srcs
        qb = q_src[qr, :]
        kb = k_src[kr, :].astype(BF16)
        vb = v_src[kr, :].astype(BF16)
        zero = jnp.zeros_like(qb)
        qs = jnp.concatenate([jnp.where(is_a, qb, zero), jnp.where(is_a, zero, qb)], axis=0).astype(BF16)
        bias = bias_ref[c] if has_prev else bias_ref[c, :, :, BLOCK:]
        lg = lax.dot_general(qs, kb, (((1,), (1,)), ((), ())), preferred_element_type=F32)
        lg = lg + bias.reshape(HEADS_PER_TILE * BLOCK, bias.shape[-1])
        m = jnp.max(lg, axis=-1, keepdims=True)
        p = jnp.exp2(lg - m)
        s = jnp.sum(p, axis=-1, keepdims=True)
        o = jnp.dot(p.astype(BF16), vb, preferred_element_type=F32)
        return tuple(jnp.where(is_a, x[:BLOCK], x[BLOCK:]) for x in (m, s, o))

    def rescale(m_old, m_new):
        e = jnp.exp2(-jnp.abs(m_old - m_new))
        keep = m_old >= m_new
        return jnp.where(keep, 1.0, e), jnp.where(keep, e, 1.0)

    def first_visit(state, qr, m_new, s_new, o_new):
        for ref, val in zip(state, (m_new, s_new, o_new)):
            ref[qr, :] = val

    def merge(state, qr, m_new, s_new, o_new):
        m_st, l_st, acc_st = state
        m_old = m_st[qr, :]
        a, b = rescale(m_old, m_new)
        m_st[qr, :] = jnp.maximum(m_old, m_new)
        l_st[qr, :] = l_st[qr, :] * a + s_new * b
        acc_st[qr, :] = acc_st[qr, :] * a + o_new * b

    def last_visit(state, qr, m_new, s_new, o_new):
        m_st, l_st, acc_st = state
        a, b = rescale(m_st[qr, :], m_new)
        o_ref[qr, :] = ((acc_st[qr, :] * a + o_new * b) / (l_st[qr, :] * a + s_new * b)).astype(o_ref.dtype)

    def run(c, srcs, units, visit, state):
        for qr, kr, has_prev in units:
            visit(state, qr, *compute(c, srcs, qr, kr, has_prev))

    def block_units(base, n_blocks):
        return [(pl.ds(base + n * BLOCK, BLOCK),
                 pl.ds(base + (n - 1) * BLOCK, 2 * BLOCK) if n else pl.ds(base, BLOCK), n > 0)
                for n in range(n_blocks)]

    nat_src, r4_src = (q_ref, k_ref, v_ref), (q4_ref, k4_ref, v4_ref)
    nat_state, r4_state = (m_ref, l_ref, acc_ref), (m4_ref, l4_ref, acc4_ref)
    to_residue_major(nat_src, r4_src)

    units = [u for r in range(4) for u in block_units(r * quarter, quarter // BLOCK)]
    run(r4_cfg, r4_src, units, first_visit, r4_state)

    assert seq == 16 * BLOCK
    units = []
    for r16 in range(16):
        rows16 = pl.ds((r16 % 4) * quarter + r16 // 4, BLOCK, stride=4)
        units.append((rows16, rows16, False))
    run(r16_cfg, r4_src, units, merge, r4_state)

    to_natural(r4_state, nat_state)
    run(nat_cfg, nat_src, block_units(0, seq // BLOCK), last_visit, nat_state)


def _attn_prompt(q, k, v, bias_vecs, w_out, w_gate, w_up, w_down, *, layer):
    b, seq, _ = q.shape
    assert all(seq % (BLOCK * dil) == 0 for _, dil in DILATED_CONFIGS)
    steps = N_HEAD_TILES * b
    bf16_rows = 16
    blk = pl.BlockSpec((None, seq, LANES), lambda t, i: (i, 0, t))

    def slab(w, every):
        rows = w.shape[1] * every // steps
        assert w.shape[1] * every % steps == 0 and rows % bf16_rows == 0
        return pl.BlockSpec((None, rows, w.shape[2]), lambda t, i: (layer, (t * b + i) // every, 0))

    def slab_out(w, every):
        rows = w.shape[1] * every // steps
        return pl.BlockSpec((rows, w.shape[2]), lambda t, i: ((t * b + i) // every, 0))

    weights = ((w_out, 1), (w_gate, 1), (w_up, 1), (w_down, 2))
    return pl.pallas_call(
        functools.partial(_attn_prompt_kernel, seq=seq),
        grid=(N_HEAD_TILES, b),
        in_specs=[blk, blk, blk,
                  pl.BlockSpec((None,) + bias_vecs.shape[1:], lambda t, i: (t, 0, 0))]
        + [slab(w, every) for w, every in weights],
        out_specs=[blk] + [slab_out(w, every) for w, every in weights],
        out_shape=[jax.ShapeDtypeStruct((b, seq, ATTN_WIDTH), BF16)]
        + [jax.ShapeDtypeStruct(w.shape[1:], BF16) for w, _ in weights],
        scratch_shapes=[pltpu.VMEM((len(DILATED_CONFIGS), HEADS_PER_TILE, BLOCK, 2 * BLOCK), F32)]
        + [pltpu.VMEM((seq, LANES), F32)] * 9,
        compiler_params=_params(("arbitrary", "arbitrary")),
        name="attn_prompt",
    )(q, k, v, bias_vecs, *(w for w, _ in weights))


def _attn_sample_kernel(q_ref, kn_ref, vn_ref, bias_ref, cnt_ref, kt_ref, vt_ref, o_ref,
                        ktail_ref, vtail_ref, *, t_new, wc):
    @pl.when(pl.program_id(0) == 0)
    def _():
        ktail_ref[...] = jnp.zeros_like(ktail_ref)
        vtail_ref[...] = jnp.zeros_like(vtail_ref)

    ktail_ref[0:t_new, :] = kn_ref[...]
    vtail_ref[0:t_new, :] = vn_ref[...]
    q = q_ref[...]
    first = lax.broadcasted_iota(jnp.int32, (HEADS_PER_TILE * t_new, 1), 0) < t_new
    contract_last = (((1,), (1,)), ((), ()))
    cnt_cache, cnt_new = cnt_ref[:, :wc], cnt_ref[:, wc:]
    for j in range(N_HEAD_TILES):
        pair = (HEADS_PER_TILE * j, HEADS_PER_TILE * j + 1)
        cols = [slice(h * HEAD_DIM, (h + 1) * HEAD_DIM) for h in pair]
        lhs = jnp.concatenate([q[:, cs] for cs in cols], axis=0).astype(BF16)
        own = lambda a, b: jnp.where(first, a, b)
        lc = own(*(jnp.dot(lhs, kt_ref[h].astype(BF16), preferred_element_type=F32) for h in pair))
        ln = own(*(lax.dot_general(lhs, ktail_ref[:, cs].astype(BF16), contract_last,
                                   preferred_element_type=F32) for cs in cols))
        lc = lc + bias_ref[j, :, :wc]
        ln = ln + bias_ref[j, :, wc:]
        m = jnp.maximum(jnp.max(lc, axis=-1, keepdims=True), jnp.max(ln, axis=-1, keepdims=True))
        pc = cnt_cache * jnp.exp2(lc - m)
        pn = cnt_new * jnp.exp2(ln - m)
        s = jnp.sum(pc, axis=-1, keepdims=True) + jnp.sum(pn, axis=-1, keepdims=True)
        pc, pn = pc.astype(BF16), pn.astype(BF16)
        for h, cs, rows in zip(pair, cols, (slice(0, t_new), slice(t_new, 2 * t_new))):
            o = (lax.dot_general(pc, vt_ref[h].astype(BF16), contract_last, preferred_element_type=F32)
                 + jnp.dot(pn, vtail_ref[:, cs].astype(BF16), preferred_element_type=F32)) / s
            o_ref[:, cs] = o[rows]


def _attn_sample(q, kn, vn, cache_kt, cache_vt, bias, cnt, *, layer):
    db, t_new, width = q.shape
    wc = cache_kt.shape[-1]
    nk = wc + BLOCK
    assert HEADS_PER_TILE * t_new == 8 and bias.shape == (N_HEAD_TILES, HEADS_PER_TILE * t_new, nk)
    assert cache_kt.shape[1:] == (db, N_HEADS, HEAD_DIM, wc)
    new_blk = pl.BlockSpec((None, t_new, width), lambda i: (i, 0, 0))
    cache_blk = pl.BlockSpec((None, None, N_HEADS, HEAD_DIM, wc), lambda i: (layer, i, 0, 0, 0))
    return pl.pallas_call(
        functools.partial(_attn_sample_kernel, t_new=t_new, wc=wc),
        grid=(db,),
        in_specs=[new_blk, new_blk, new_blk,
                  _resident(bias.shape, lambda i: (0, 0, 0)), _resident(cnt.shape, lambda i: (0, 0)),
                  cache_blk, cache_blk],
        out_specs=new_blk,
        out_shape=jax.ShapeDtypeStruct((db, t_new, width), F32),
        scratch_shapes=[pltpu.VMEM((BLOCK, width), F32)] * 2,
        compiler_params=_params(("arbitrary",)),
        name="attn_sample",
    )(q, kn, vn, bias, cnt, cache_kt, cache_vt)


def _out_proj_kernel(x_ref, a_ref, p_ref, w_ref, o_ref, *, tn):
    a = a_ref[...].astype(BF16)
    p = p_ref[...]
    for c in range(o_ref.shape[1] // tn):
        cs = slice(c * tn, (c + 1) * tn)
        mixed = (jnp.dot(a, w_ref[0:ATTN_WIDTH, cs], preferred_element_type=F32)
                 + jnp.dot(p, w_ref[ATTN_WIDTH:, cs], preferred_element_type=F32))
        o_ref[:, cs] = x_ref[:, cs] + mixed


def _out_proj(x, attn, pool, w_out, *, tm):
    m, d = x.shape
    row_block = lambda width: pl.BlockSpec((tm, width), lambda i: (i, 0))
    return pl.pallas_call(
        functools.partial(_out_proj_kernel, tn=512),
        grid=(m // tm,),
        in_specs=[row_block(d), row_block(ATTN_WIDTH), row_block(POOL_WIDTH),
                  _resident(w_out.shape, lambda i: (0, 0))],
        out_specs=row_block(d),
        out_shape=jax.ShapeDtypeStruct((m, d), F32),
        compiler_params=_params(("arbitrary",)),
        name="out_proj",
    )(x, attn, pool, w_out)


def _ffn_kernel(x_ref, g_ref, wg_ref, wu_ref, wd_ref, gf_ref, o_ref, h_ref, *, final_norm):
    j = pl.program_id(1)

    @pl.when(j == 0)
    def _():
        xf = x_ref[...]
        h_ref[...] = _rmsnorm_rows(xf, g_ref[...]).astype(BF16)
        o_ref[...] = xf

    h = h_ref[...]
    half = wg_ref.shape[1] // 2
    partial = None
    for c in range(2):
        cs = slice(c * half, (c + 1) * half)
        gate = jnp.dot(h, wg_ref[:, cs], preferred_element_type=F32)
        up = jnp.dot(h, wu_ref[:, cs], preferred_element_type=F32)
        act = (gate * jax.nn.sigmoid(gate) * up).astype(BF16)
        down = jnp.dot(act, wd_ref[cs, :], preferred_element_type=F32)
        partial = down if partial is None else partial + down
    o_ref[...] += partial

    if final_norm:
        @pl.when(j == pl.num_programs(1) - 1)
        def _():
            o_ref[...] = _rmsnorm_rows(o_ref[...], gf_ref[...])


def _ffn(x, g, w_gate, w_up, w_down, g_final, *, tm, tf, final_norm):
    m, d = x.shape
    f = w_gate.shape[1]
    assert m % tm == 0 and f % tf == 0
    return pl.pallas_call(
        functools.partial(_ffn_kernel, final_norm=final_norm),
        grid=(m // tm, f // tf),
        in_specs=[pl.BlockSpec((tm, d), lambda i, j: (i, 0)),
                  _resident((1, d), lambda i, j: (0, 0)),
                  pl.BlockSpec((d, tf), lambda i, j: (0, j)),
                  pl.BlockSpec((d, tf), lambda i, j: (0, j)),
                  pl.BlockSpec((tf, d), lambda i, j: (j, 0)),
                  _resident((1, d), lambda i, j: (0, 0))],
        out_specs=pl.BlockSpec((tm, d), lambda i, j: (i, 0)),
        out_shape=jax.ShapeDtypeStruct((m, d), F32),
        scratch_shapes=[pltpu.VMEM((tm, d), BF16)],
        compiler_params=_params(("arbitrary", "arbitrary")),
        name="ffn",
    )(x, g, w_gate, w_up, w_down, g_final)


def _t5_bucket(dist):
    max_exact = NUM_BUCKETS // 2
    df = jnp.maximum(dist, 1).astype(F32)
    large = max_exact + (jnp.log(df / max_exact) / math.log(MAX_WINDOW / max_exact)
                         * (NUM_BUCKETS - max_exact)).astype(jnp.int32)
    large = jnp.minimum(large, NUM_BUCKETS - 1)
    return jnp.where(dist < max_exact, dist, large)


BIAS_PERIOD = 3 * BLOCK


def _prompt_bias_vectors(rel_bias):
    vecs = []
    for window, dil in DILATED_CONFIGS:
        sub_w = window // dil
        assert sub_w <= BLOCK
        bias_sub = rel_bias[_t5_bucket(dil * jnp.arange(sub_w + 1))].T.astype(F32) * LOG2E
        vec = jnp.full((N_HEADS, BIAS_PERIOD), -jnp.inf, F32)
        vecs.append(vec.at[:, BLOCK - sub_w:BLOCK + 1].set(bias_sub[:, ::-1]))
    vecs = jnp.stack(vecs, axis=1)
    vecs = vecs.reshape(N_HEAD_TILES, HEADS_PER_TILE, len(DILATED_CONFIGS), BIAS_PERIOD)
    return vecs.transpose(0, 2, 1, 3).reshape(N_HEAD_TILES, -1, BIAS_PERIOD)


def _sample_tables(rel_bias, wc, t_new):
    nk = wc + BLOCK
    tail = np.where(np.arange(BLOCK) < t_new, wc + np.arange(BLOCK), -1)
    key_pos = np.concatenate([np.arange(wc), tail])
    dist = wc + np.arange(t_new)[:, None] - key_pos[None, :]
    listed = (key_pos >= 0)[None, :] & (dist >= 0)
    cnt = sum((listed & (dist <= w) & (dist % d == 0)).astype(np.float32) for w, d in DILATED_CONFIGS)
    max_dist = wc + t_new - 1
    by_dist = rel_bias[_t5_bucket(jnp.arange(max_dist + 1))].T.astype(F32) * LOG2E
    rev = jnp.pad(by_dist[:, ::-1], ((0, 0), (0, nk)))
    bias = jnp.stack([rev[:, t_new - 1 - t:t_new - 1 - t + nk] for t in range(t_new)], axis=1)
    bias = jnp.where(jnp.asarray(cnt > 0)[None], bias, -jnp.inf)
    bias = bias.reshape(N_HEAD_TILES, HEADS_PER_TILE * t_new, nk)
    return bias, jnp.asarray(np.tile(cnt, (HEADS_PER_TILE, 1)))


def kernel(x_prompt, x_sample, cache_k, cache_v, state_pool, rel_bias, norm_mix, w_in, w_pool,
           pool_scale, w_out, norm_ffn, w_gate, w_up, w_down, norm_final):
    b, seq, d = x_prompt.shape
    db, t_new, _ = x_sample.shape
    depth = w_in.shape[0]
    wc = cache_k.shape[2]
    assert depth >= 1 and wc == MAX_WINDOW and seq >= MAX_WINDOW and t_new <= POOL_STATE

    tm_prompt = 512
    m_sample = db * t_new
    gf = norm_final.reshape(1, d)
    prompt_bias = _prompt_bias_vectors(rel_bias)
    sample_bias, sample_cnt = _sample_tables(rel_bias, wc, t_new)
    cache_kt, cache_vt = (c.transpose(0, 1, 3, 4, 2) for c in (cache_k, cache_v))
    zero_hist = jnp.zeros((POOL_HIST, POOL_WIDTH), F32)

    def to_tok_major(a):
        return a.transpose(1, 0, 2).reshape(a.shape[1] * db, a.shape[2])

    def to_seq_major(a):
        return a.reshape(a.shape[0] // db, db, a.shape[1]).transpose(1, 0, 2)

    xp = x_prompt.reshape(b * seq, d)
    xs = to_tok_major(x_sample)
    outs = {name: [] for name in ("kp", "vp", "pp", "ks", "vs", "ps")}
    for l in range(depth):
        g_mix = norm_mix[l].reshape(1, d)
        g_ffn = norm_ffn[l].reshape(1, d)
        w_in_l = w_in[l].astype(BF16)
        w_pool_l = w_pool[l].astype(BF16)
        scale_l = pool_scale[l].reshape(1, POOL_WIDTH)
        last = l == depth - 1

        tiles_per_seq = seq // tm_prompt
        q, k, v, kt, vt, pool, utail = _in_proj(
            xp, g_mix, w_in_l, zero_hist, w_pool_l, scale_l, tm=tm_prompt, tiles_per_seq=tiles_per_seq,
            shift=1, n_hist=0, tail_rows=POOL_HIST)
        attn, w_out_l, w_gate_l, w_up_l, w_down_l = _attn_prompt(
            q.reshape(b, seq, -1), k.reshape(b, seq, -1), v.reshape(b, seq, -1), prompt_bias,
            w_out, w_gate, w_up, w_down, layer=l)
        xp = _out_proj(xp, attn.reshape(b * seq, -1), pool, w_out_l, tm=2 * tm_prompt)
        xp = _ffn(xp, g_ffn, w_gate_l, w_up_l, w_down_l, gf, tm=2 * tm_prompt, tf=512, final_norm=last)
        win = min(MAX_WINDOW, seq)
        outs["kp"].append(kt.transpose(0, 3, 1, 2)[:, -win:])
        outs["vp"].append(vt.transpose(0, 3, 1, 2)[:, -win:])
        outs["pp"].append(utail.reshape(b, tiles_per_seq, POOL_HIST, -1)[:, -1, -POOL_STATE:])

        hist = jnp.concatenate([jnp.zeros((db, POOL_WIDTH), F32), to_tok_major(state_pool[l])], axis=0)
        q, k, v, _, _, pool, utail = _in_proj(
            xs, g_mix, w_in_l, hist, w_pool_l, scale_l, tm=m_sample, tiles_per_seq=1, shift=db,
            n_hist=POOL_STATE, tail_rows=m_sample)
        q, k, v, u = (to_seq_major(a) for a in (q, k, v, utail[0]))
        attn = _attn_sample(q, k, v, cache_kt, cache_vt, sample_bias, sample_cnt, layer=l)
        xs = _out_proj(xs, to_tok_major(attn), pool, w_out_l, tm=m_sample)
        xs = _ffn(xs, g_ffn, w_gate_l, w_up_l, w_down_l, gf, tm=m_sample, tf=512, final_norm=last)
        outs["ks"].append(k.reshape(db, t_new, N_HEADS, HEAD_DIM))
        outs["vs"].append(v.reshape(db, t_new, N_HEADS, HEAD_DIM))
        outs["ps"].append(jnp.concatenate([state_pool[l], u], axis=1)[:, -POOL_STATE:])

    y_prompt = xp.reshape(b, seq, d)
    y_sample = to_seq_major(xs)
    return (y_prompt, y_sample, jnp.stack(outs["kp"]), jnp.stack(outs["vp"]), jnp.stack(outs["pp"]),
            jnp.stack(outs["ks"]), jnp.stack(outs["vs"]), jnp.stack(outs["ps"]))
```

```python
import functools
import math

import numpy as np

import jax
import jax.numpy as jnp
from jax import lax
from jax.experimental import pallas as pl
from jax.experimental.pallas import tpu as pltpu

HEAD_DIM = 64
N_HEADS = 16
ATTN_WIDTH = N_HEADS * HEAD_DIM
DILATED_CONFIGS = ((128, 1), (512, 4), (2048, 16))
MAX_WINDOW = 2048
BLOCK = 128
POOL_WINDOWS = (2, 4, 8, 16)
POOL_GROUP_DIM = 256
POOL_WIDTH = len(POOL_WINDOWS) * POOL_GROUP_DIM
POOL_STATE = max(POOL_WINDOWS) - 1
POOL_HIST = max(POOL_WINDOWS)
NUM_BUCKETS = 32
EPS = 1e-6
SCALE = HEAD_DIM ** -0.5
LOG2E = math.log2(math.e)

LANES = 128
HEADS_PER_TILE = LANES // HEAD_DIM
N_HEAD_TILES = N_HEADS // HEADS_PER_TILE
VMEM_LIMIT = 56 * 1024 * 1024

F32 = jnp.float32
BF16 = jnp.bfloat16


def _params(semantics):
    return pltpu.CompilerParams(dimension_semantics=semantics, vmem_limit_bytes=VMEM_LIMIT)


def _resident(shape, index_map):
    return pl.BlockSpec(shape, index_map, pipeline_mode=pl.Buffered(1))


def _rmsnorm_rows(xf, g):
    return xf * lax.rsqrt(jnp.mean(xf * xf, axis=-1, keepdims=True) + EPS) * g


def _in_proj_kernel(x_ref, g_ref, w_ref, hist_ref, wp_ref, ps_ref, *rest,
                    tm, tn, tiles_per_seq, shift, n_hist, side_heads):
    if side_heads:
        side_in, rest = rest[:7], rest[7:]
        (q_ref, k_ref, v_ref, kt_ref, vt_ref, pool_ref, utail_ref, so_ref,
         h_ref, ucat_ref, pooled_ref, ktail_ref, vtail_ref) = rest
        side = _SampleAttnRefs(*side_in, so_ref, ktail_ref, vtail_ref)
        pairs_per_step = side_heads // HEADS_PER_TILE
        scores = _sample_attn_scores(side, (pl.program_id(0) % (N_HEADS // side_heads)) * pairs_per_step)
    else:
        q_ref, k_ref, v_ref, kt_ref, vt_ref, pool_ref, utail_ref, h_ref, ucat_ref, pooled_ref = rest
    seq_tile = pl.program_id(0) % tiles_per_seq
    hist_rows = POOL_HIST * shift
    tail_rows = utail_ref.shape[0]

    @pl.when(seq_tile == 0)
    def _():
        ucat_ref[0:hist_rows, :] = hist_ref[...]

    h_ref[...] = _rmsnorm_rows(x_ref[...], g_ref[...]).astype(BF16)
    per_out = ATTN_WIDTH // tn
    heads_per_chunk = tn // HEAD_DIM
    for c in list(range(3 * per_out, 4 * per_out)) + list(range(3 * per_out)):
        res = jnp.dot(h_ref[...], w_ref[:, c * tn:(c + 1) * tn], preferred_element_type=F32)
        which, part = divmod(c, per_out)
        cs = slice(part * tn, (part + 1) * tn)
        if which == 0:
            q_ref[:, cs] = res * (SCALE * LOG2E)
        elif which == 3:
            ucat_ref[hist_rows:hist_rows + tm, cs] = res
        else:
            nat, tr = ((k_ref, kt_ref), (v_ref, vt_ref))[which - 1]
            nat[:, cs] = res
            heads = slice(part * heads_per_chunk, (part + 1) * heads_per_chunk)
            tr[heads] = res.T.reshape(heads_per_chunk, HEAD_DIM, tm)

    utail_ref[...] = ucat_ref[hist_rows + tm - tail_rows:hist_rows + tm, :]

    chunk = min(128, tm)
    for r0 in range(0, tm, chunk):
        row = lax.broadcasted_iota(jnp.int32, (chunk, 1), 0)
        tok = (seq_tile * tm + r0 + row) // shift
        for g, w in enumerate(POOL_WINDOWS):
            cs = slice(g * POOL_GROUP_DIM, (g + 1) * POOL_GROUP_DIM)
            u0 = ucat_ref[hist_rows + r0:hist_rows + r0 + chunk, cs]
            acc = u0
            for j in range(1, w):
                lo = hist_rows + r0 - j * shift
                acc = acc + ucat_ref[lo:lo + chunk, cs]
            cnt = jnp.minimum(tok + (n_hist + 1), w).astype(F32)
            pooled_ref[r0:r0 + chunk, cs] = (acc / cnt - u0).astype(BF16)

    for g in range(len(POOL_WINDOWS)):
        cs = slice(g * POOL_GROUP_DIM, (g + 1) * POOL_GROUP_DIM)
        y = jnp.dot(pooled_ref[:, cs], wp_ref[g], preferred_element_type=F32) * ps_ref[:, cs]
        pool_ref[:, cs] = y.astype(BF16)

    if tiles_per_seq > 1:
        ucat_ref[0:hist_rows, :] = ucat_ref[tm:tm + hist_rows, :]

    if side_heads:
        _sample_attn_values(side, scores)


def _in_proj(x, g, w_in, hist, w_pool, pool_scale, *, tm, tiles_per_seq, shift, n_hist, tail_rows,
             sample=None, layer=0):
    m, d = x.shape
    in_cols = w_in.shape[1]
    hist_rows = POOL_HIST * shift
    n_tiles = m // tm
    assert m % tm == 0 and n_tiles % tiles_per_seq == 0 and hist.shape == (hist_rows, POOL_WIDTH)
    assert tiles_per_seq == 1 or tm >= hist_rows
    row_block = lambda width: pl.BlockSpec((tm, width), lambda i: (i, 0))
    tr_block = pl.BlockSpec((None, N_HEADS, HEAD_DIM, tm),
                            lambda i: (i // tiles_per_seq, 0, 0, i % tiles_per_seq))
    tr_shape = jax.ShapeDtypeStruct((n_tiles // tiles_per_seq, N_HEADS, HEAD_DIM, tiles_per_seq * tm), F32)
    in_specs = [
        row_block(d),
        _resident((1, d), lambda i: (0, 0)),
        _resident((d, in_cols), lambda i: (0, 0)),
        _resident((hist_rows, POOL_WIDTH), lambda i: (0, 0)),
        _resident(w_pool.shape, lambda i: (0, 0, 0)),
        _resident((1, POOL_WIDTH), lambda i: (0, 0)),
    ]
    out_specs = [row_block(ATTN_WIDTH)] * 3 + [tr_block] * 2 + [
        row_block(POOL_WIDTH), pl.BlockSpec((None, tail_rows, POOL_WIDTH), lambda i: (i, 0, 0))]
    out_shape = [jax.ShapeDtypeStruct((m, ATTN_WIDTH), F32)] * 3 + [tr_shape] * 2 + [
        jax.ShapeDtypeStruct((m, POOL_WIDTH), BF16),
        jax.ShapeDtypeStruct((n_tiles, tail_rows, POOL_WIDTH), F32)]
    scratch = [
        pltpu.VMEM((tm, d), BF16),
        pltpu.VMEM((hist_rows + tm, POOL_WIDTH), F32),
        pltpu.VMEM((tm, POOL_WIDTH), BF16),
    ]
    operands = [x, g, w_in, hist, w_pool, pool_scale]
    side_heads = 0
    if sample is not None:
        sq, skn, svn, sbias, scnt, ckt, cvt = sample
        db, t_new, _ = sq.shape
        wc = ckt.shape[-1]
        steps_per_seq = n_tiles // db
        assert n_tiles == db * steps_per_seq and N_HEADS % steps_per_seq == 0
        side_heads = N_HEADS // steps_per_seq
        width = side_heads * HEAD_DIM
        assert side_heads % HEADS_PER_TILE == 0 and width % LANES == 0 and HEADS_PER_TILE * t_new == 8
        assert sbias.shape == (N_HEAD_TILES, HEADS_PER_TILE * t_new, wc + BLOCK)
        assert ckt.shape[1:] == (db, N_HEADS, HEAD_DIM, wc)
        new_blk = pl.BlockSpec((None, t_new, width), lambda i: (i // steps_per_seq, 0, i % steps_per_seq))
        cache_blk = pl.BlockSpec((None, None, side_heads, HEAD_DIM, wc),
                                 lambda i: (layer, i // steps_per_seq, i % steps_per_seq, 0, 0))
        in_specs += [new_blk] * 3 + [_resident(sbias.shape, lambda i: (0, 0, 0)),
                                     _resident(scnt.shape, lambda i: (0, 0)), cache_blk, cache_blk]
        out_specs.append(new_blk)
        out_shape.append(jax.ShapeDtypeStruct(sq.shape, F32))
        scratch += [pltpu.VMEM((BLOCK, width), F32)] * 2
        operands += [sq, skn, svn, sbias, scnt, ckt, cvt]
    kern = functools.partial(_in_proj_kernel, tm=tm, tn=512, tiles_per_seq=tiles_per_seq,
                             shift=shift, n_hist=n_hist, side_heads=side_heads)
    return pl.pallas_call(
        kern,
        grid=(n_tiles,),
        in_specs=in_specs,
        out_specs=out_specs,
        out_shape=out_shape,
        scratch_shapes=scratch,
        compiler_params=_params(("arbitrary",)),
        name="in_proj",
    )(*operands)


def _attn_prompt_kernel(q_ref, k_ref, v_ref, bvec_ref, wo_ref, wg_ref, wu_ref, wd_ref,
                        o_ref, wo_bf_ref, wg_bf_ref, wu_bf_ref, wd_bf_ref, bias_ref,
                        q4_ref, k4_ref, v4_ref, m4_ref, l4_ref, acc4_ref, m_ref, l_ref, acc_ref, *, seq):
    step = pl.program_id(0) * pl.num_programs(1) + pl.program_id(1)
    for src, dst in ((wo_ref, wo_bf_ref), (wg_ref, wg_bf_ref), (wu_ref, wu_bf_ref)):
        dst[...] = src[...].astype(BF16)

    @pl.when(step % 2 == 0)
    def _():
        wd_bf_ref[...] = wd_ref[...].astype(BF16)

    @pl.when(pl.program_id(1) == 0)
    def _():
        for c in range(len(DILATED_CONFIGS)):
            for hh in range(HEADS_PER_TILE):
                vec = bvec_ref[c * HEADS_PER_TILE + hh:c * HEADS_PER_TILE + hh + 1, :]
                rows = jnp.broadcast_to(vec, (BLOCK, BIAS_PERIOD))
                bias_ref[c, hh] = pltpu.roll(rows, 0, 1, stride=1, stride_axis=0)[:, :2 * BLOCK]

    is_a = lax.broadcasted_iota(jnp.int32, (BLOCK, LANES), 1) < HEAD_DIM
    quarter = seq // 4
    nat_cfg, r4_cfg, r16_cfg = range(3)
    assert DILATED_CONFIGS[r4_cfg][1] == 4 and DILATED_CONFIGS[r16_cfg][1] == 16

    def for_chunks(fn):
        for r in range(4):
            for j in range(quarter // BLOCK):
                fn(pl.ds(r + 4 * BLOCK * j, BLOCK, stride=4), pl.ds(r * quarter + BLOCK * j, BLOCK))

    def to_residue_major(nat, r4):
        for src, dst in zip(nat, r4):
            def move(nat_rows, r4_rows, src=src, dst=dst):
                dst[r4_rows, :] = src[nat_rows, :]
            for_chunks(move)

    def to_natural(r4, nat):
        for src, dst in zip(r4, nat):
            def move(nat_rows, r4_rows, src=src, dst=dst):
                dst[nat_rows, :] = src[r4_rows, :]
            for_chunks(move)

    def compute(c, srcs, qr, kr, has_prev):
        q_src, k_src, v_src = srcs
        qb = q_src[qr, :]
        kb = k_src[kr, :].astype(BF16)
        vb = v_src[kr, :].astype(BF16)
        zero = jnp.zeros_like(qb)
        qs = jnp.concatenate([jnp.where(is_a, qb, zero), jnp.where(is_a, zero, qb)], axis=0).astype(BF16)
        bias = bias_ref[c] if has_prev else bias_ref[c, :, :, BLOCK:]
        lg = lax.dot_general(qs, kb, (((1,), (1,)), ((), ())), preferred_element_type=F32)
        lg = lg + bias.reshape(HEADS_PER_TILE * BLOCK, bias.shape[-1])
        m = jnp.max(lg, axis=-1, keepdims=True)
        p = jnp.exp2(lg - m)
        s = jnp.sum(p, axis=-1, keepdims=True)
        o = jnp.dot(p.astype(BF16), vb, preferred_element_type=F32)
        return tuple(jnp.where(is_a, x[:BLOCK], x[BLOCK:]) for x in (m, s, o))

    def rescale(m_old, m_new):
        e = jnp.exp2(-jnp.abs(m_old - m_new))
        keep = m_old >= m_new
        return jnp.where(keep, 1.0, e), jnp.where(keep, e, 1.0)

    def first_visit(state, qr, m_new, s_new, o_new):
        for ref, val in zip(state, (m_new, s_new, o_new)):
            ref[qr, :] = val

    def merge(state, qr, m_new, s_new, o_new):
        m_st, l_st, acc_st = state
        m_old = m_st[qr, :]
        a, b = rescale(m_old, m_new)
        m_st[qr, :] = jnp.maximum(m_old, m_new)
        l_st[qr, :] = l_st[qr, :] * a + s_new * b
        acc_st[qr, :] = acc_st[qr, :] * a + o_new * b

    def last_visit(state, qr, m_new, s_new, o_new):
        m_st, l_st, acc_st = state
        a, b = rescale(m_st[qr, :], m_new)
        o_ref[qr, :] = ((acc_st[qr, :] * a + o_new * b) / (l_st[qr, :] * a + s_new * b)).astype(o_ref.dtype)

    def run(c, srcs, units, visit, state):
        for qr, kr, has_prev in units:
            visit(state, qr, *compute(c, srcs, qr, kr, has_prev))

    def block_units(base, n_blocks):
        return [(pl.ds(base + n * BLOCK, BLOCK),
                 pl.ds(base + (n - 1) * BLOCK, 2 * BLOCK) if n else pl.ds(base, BLOCK), n > 0)
                for n in range(n_blocks)]

    nat_src, r4_src = (q_ref, k_ref, v_ref), (q4_ref, k4_ref, v4_ref)
    nat_state, r4_state = (m_ref, l_ref, acc_ref), (m4_ref, l4_ref, acc4_ref)
    to_residue_major(nat_src, r4_src)

    units = [u for r in range(4) for u in block_units(r * quarter, quarter // BLOCK)]
    run(r4_cfg, r4_src, units, first_visit, r4_state)

    assert seq == 16 * BLOCK
    units = []
    for r16 in range(16):
        rows16 = pl.ds((r16 % 4) * quarter + r16 // 4, BLOCK, stride=4)
        units.append((rows16, rows16, False))
    run(r16_cfg, r4_src, units, merge, r4_state)

    to_natural(r4_state, nat_state)
    run(nat_cfg, nat_src, block_units(0, seq // BLOCK), last_visit, nat_state)


def _attn_prompt(q, k, v, bias_vecs, w_out, w_gate, w_up, w_down, *, layer):
    b, seq, _ = q.shape
    assert all(seq % (BLOCK * dil) == 0 for _, dil in DILATED_CONFIGS)
    steps = N_HEAD_TILES * b
    bf16_rows = 16
    blk = pl.BlockSpec((None, seq, LANES), lambda t, i: (i, 0, t))

    def slab(w, every):
        rows = w.shape[1] * every // steps
        assert w.shape[1] * every % steps == 0 and rows % bf16_rows == 0
        return pl.BlockSpec((None, rows, w.shape[2]), lambda t, i: (layer, (t * b + i) // every, 0))

    def slab_out(w, every):
        rows = w.shape[1] * every // steps
        return pl.BlockSpec((rows, w.shape[2]), lambda t, i: ((t * b + i) // every, 0))

    weights = ((w_out, 1), (w_gate, 1), (w_up, 1), (w_down, 2))
    return pl.pallas_call(
        functools.partial(_attn_prompt_kernel, seq=seq),
        grid=(N_HEAD_TILES, b),
        in_specs=[blk, blk, blk,
                  pl.BlockSpec((None,) + bias_vecs.shape[1:], lambda t, i: (t, 0, 0))]
        + [slab(w, every) for w, every in weights],
        out_specs=[blk] + [slab_out(w, every) for w, every in weights],
        out_shape=[jax.ShapeDtypeStruct((b, seq, ATTN_WIDTH), BF16)]
        + [jax.ShapeDtypeStruct(w.shape[1:], BF16) for w, _ in weights],
        scratch_shapes=[pltpu.VMEM((len(DILATED_CONFIGS), HEADS_PER_TILE, BLOCK, 2 * BLOCK), F32)]
        + [pltpu.VMEM((seq, LANES), F32)] * 9,
        compiler_params=_params(("arbitrary", "arbitrary")),
        name="attn_prompt",
    )(q, k, v, bias_vecs, *(w for w, _ in weights))


class _SampleAttnRefs:
    def __init__(self, q, kn, vn, bias, cnt, kt, vt, o, ktail, vtail):
        self.q, self.kn, self.vn, self.bias, self.cnt = q, kn, vn, bias, cnt
        self.kt, self.vt, self.o, self.ktail, self.vtail = kt, vt, o, ktail, vtail


_CONTRACT_LAST = (((1,), (1,)), ((), ()))


def _sample_attn_scores(r, first_pair):
    t_new, wc = r.q.shape[0], r.kt.shape[-1]
    r.ktail[t_new:, :] = jnp.zeros((BLOCK - t_new, r.ktail.shape[1]), F32)
    r.vtail[t_new:, :] = jnp.zeros((BLOCK - t_new, r.vtail.shape[1]), F32)
    r.ktail[0:t_new, :] = r.kn[...]
    r.vtail[0:t_new, :] = r.vn[...]
    q = r.q[...]
    first = lax.broadcasted_iota(jnp.int32, (HEADS_PER_TILE * t_new, 1), 0) < t_new
    cnt_cache, cnt_new = r.cnt[:, :wc], r.cnt[:, wc:]
    out = []
    for j in range(r.kt.shape[0] // HEADS_PER_TILE):
        pair = (HEADS_PER_TILE * j, HEADS_PER_TILE * j + 1)
        cols = [slice(h * HEAD_DIM, (h + 1) * HEAD_DIM) for h in pair]
        lhs = jnp.concatenate([q[:, cs] for cs in cols], axis=0).astype(BF16)
        own = lambda a, b: jnp.where(first, a, b)
        lc = own(*(jnp.dot(lhs, r.kt[h].astype(BF16), preferred_element_type=F32) for h in pair))
        ln = own(*(lax.dot_general(lhs, r.ktail[:, cs].astype(BF16), _CONTRACT_LAST,
                                   preferred_element_type=F32) for cs in cols))
        bias = r.bias[first_pair + j]
        lc = lc + bias[:, :wc]
        ln = ln + bias[:, wc:]
        m = jnp.maximum(jnp.max(lc, axis=-1, keepdims=True), jnp.max(ln, axis=-1, keepdims=True))
        pc = cnt_cache * jnp.exp2(lc - m)
        pn = cnt_new * jnp.exp2(ln - m)
        s = jnp.sum(pc, axis=-1, keepdims=True) + jnp.sum(pn, axis=-1, keepdims=True)
        out.append((pair, cols, pc.astype(BF16), pn.astype(BF16), s))
    return out


def _sample_attn_values(r, scores):
    t_new = r.q.shape[0]
    for pair, cols, pc, pn, s in scores:
        for h, cs, rows in zip(pair, cols, (slice(0, t_new), slice(t_new, 2 * t_new))):
            o = (lax.dot_general(pc, r.vt[h].astype(BF16), _CONTRACT_LAST, preferred_element_type=F32)
                 + jnp.dot(pn, r.vtail[:, cs].astype(BF16), preferred_element_type=F32)) / s
            r.o[:, cs] = o[rows]


def _out_proj_kernel(x_ref, a_ref, p_ref, w_ref, o_ref, *, tn):
    a = a_ref[...].astype(BF16)
    p = p_ref[...]
    for c in range(o_ref.shape[1] // tn):
        cs = slice(c * tn, (c + 1) * tn)
        mixed = (jnp.dot(a, w_ref[0:ATTN_WIDTH, cs], preferred_element_type=F32)
                 + jnp.dot(p, w_ref[ATTN_WIDTH:, cs], preferred_element_type=F32))
        o_ref[:, cs] = x_ref[:, cs] + mixed


def _out_proj(x, attn, pool, w_out, *, tm):
    m, d = x.shape
    row_block = lambda width: pl.BlockSpec((tm, width), lambda i: (i, 0))
    return pl.pallas_call(
        functools.partial(_out_proj_kernel, tn=512),
        grid=(m // tm,),
        in_specs=[row_block(d), row_block(ATTN_WIDTH), row_block(POOL_WIDTH),
                  _resident(w_out.shape, lambda i: (0, 0))],
        out_specs=row_block(d),
        out_shape=jax.ShapeDtypeStruct((m, d), F32),
        compiler_params=_params(("arbitrary",)),
        name="out_proj",
    )(x, attn, pool, w_out)


def _ffn_kernel(x_ref, g_ref, wg_ref, wu_ref, wd_ref, gf_ref, o_ref, h_ref, *, final_norm):
    j = pl.program_id(1)

    @pl.when(j == 0)
    def _():
        xf = x_ref[...]
        h_ref[...] = _rmsnorm_rows(xf, g_ref[...]).astype(BF16)
        o_ref[...] = xf

    h = h_ref[...]
    half = wg_ref.shape[1] // 2
    partial = None
    for c in range(2):
        cs = slice(c * half, (c + 1) * half)
        gate = jnp.dot(h, wg_ref[:, cs], preferred_element_type=F32)
        up = jnp.dot(h, wu_ref[:, cs], preferred_element_type=F32)
        act = (gate * jax.nn.sigmoid(gate) * up).astype(BF16)
        down = jnp.dot(act, wd_ref[cs, :], preferred_element_type=F32)
        partial = down if partial is None else partial + down
    o_ref[...] += partial

    if final_norm:
        @pl.when(j == pl.num_programs(1) - 1)
        def _():
            o_ref[...] = _rmsnorm_rows(o_ref[...], gf_ref[...])


def _ffn(x, g, w_gate, w_up, w_down, g_final, *, tm, tf, final_norm):
    m, d = x.shape
    f = w_gate.shape[1]
    assert m % tm == 0 and f % tf == 0
    return pl.pallas_call(
        functools.partial(_ffn_kernel, final_norm=final_norm),
        grid=(m // tm, f // tf),
        in_specs=[pl.BlockSpec((tm, d), lambda i, j: (i, 0)),
                  _resident((1, d), lambda i, j: (0, 0)),
                  pl.BlockSpec((d, tf), lambda i, j: (0, j)),
                  pl.BlockSpec((d, tf), lambda i, j: (0, j)),
                  pl.BlockSpec((tf, d), lambda i, j: (j, 0)),
                  _resident((1, d), lambda i, j: (0, 0))],
        out_specs=pl.BlockSpec((tm, d), lambda i, j: (i, 0)),
        out_shape=jax.ShapeDtypeStruct((m, d), F32),
        scratch_shapes=[pltpu.VMEM((tm, d), BF16)],
        compiler_params=_params(("arbitrary", "arbitrary")),
        name="ffn",
    )(x, g, w_gate, w_up, w_down, g_final)


def _t5_bucket(dist):
    max_exact = NUM_BUCKETS // 2
    df = jnp.maximum(dist, 1).astype(F32)
    large = max_exact + (jnp.log(df / max_exact) / math.log(MAX_WINDOW / max_exact)
                         * (NUM_BUCKETS - max_exact)).astype(jnp.int32)
    large = jnp.minimum(large, NUM_BUCKETS - 1)
    return jnp.where(dist < max_exact, dist, large)


BIAS_PERIOD = 3 * BLOCK


def _prompt_bias_vectors(rel_bias):
    vecs = []
    for window, dil in DILATED_CONFIGS:
        sub_w = window // dil
        assert sub_w <= BLOCK
        bias_sub = rel_bias[_t5_bucket(dil * jnp.arange(sub_w + 1))].T.astype(F32) * LOG2E
        vec = jnp.full((N_HEADS, BIAS_PERIOD), -jnp.inf, F32)
        vecs.append(vec.at[:, BLOCK - sub_w:BLOCK + 1].set(bias_sub[:, ::-1]))
    vecs = jnp.stack(vecs, axis=1)
    vecs = vecs.reshape(N_HEAD_TILES, HEADS_PER_TILE, len(DILATED_CONFIGS), BIAS_PERIOD)
    return vecs.transpose(0, 2, 1, 3).reshape(N_HEAD_TILES, -1, BIAS_PERIOD)


def _sample_tables(rel_bias, wc, t_new):
    nk = wc + BLOCK
    tail = np.where(np.arange(BLOCK) < t_new, wc + np.arange(BLOCK), -1)
    key_pos = np.concatenate([np.arange(wc), tail])
    dist = wc + np.arange(t_new)[:, None] - key_pos[None, :]
    listed = (key_pos >= 0)[None, :] & (dist >= 0)
    cnt = sum((listed & (dist <= w) & (dist % d == 0)).astype(np.float32) for w, d in DILATED_CONFIGS)
    max_dist = wc + t_new - 1
    by_dist = rel_bias[_t5_bucket(jnp.arange(max_dist + 1))].T.astype(F32) * LOG2E
    rev = jnp.pad(by_dist[:, ::-1], ((0, 0), (0, nk)))
    bias = jnp.stack([rev[:, t_new - 1 - t:t_new - 1 - t + nk] for t in range(t_new)], axis=1)
    bias = jnp.where(jnp.asarray(cnt > 0)[None], bias, -jnp.inf)
    bias = bias.reshape(N_HEAD_TILES, HEADS_PER_TILE * t_new, nk)
    return bias, jnp.asarray(np.tile(cnt, (HEADS_PER_TILE, 1)))


def kernel(x_prompt, x_sample, cache_k, cache_v, state_pool, rel_bias, norm_mix, w_in, w_pool,
           pool_scale, w_out, norm_ffn, w_gate, w_up, w_down, norm_final):
    b, seq, d = x_prompt.shape
    db, t_new, _ = x_sample.shape
    depth = w_in.shape[0]
    wc = cache_k.shape[2]
    assert depth >= 1 and wc == MAX_WINDOW and seq >= MAX_WINDOW and t_new <= POOL_STATE

    tm_prompt = 512
    tm_in = b * seq // (2 * db)
    m_sample = db * t_new
    gf = norm_final.reshape(1, d)
    prompt_bias = _prompt_bias_vectors(rel_bias)
    sample_bias, sample_cnt = _sample_tables(rel_bias, wc, t_new)
    cache_kt, cache_vt = (c.transpose(0, 1, 3, 4, 2) for c in (cache_k, cache_v))
    zero_hist = jnp.zeros((POOL_HIST, POOL_WIDTH), F32)

    def to_tok_major(a):
        return a.transpose(1, 0, 2).reshape(a.shape[1] * db, a.shape[2])

    def to_seq_major(a):
        return a.reshape(a.shape[0] // db, db, a.shape[1]).transpose(1, 0, 2)

    xp = x_prompt.reshape(b * seq, d)
    xs = to_tok_major(x_sample)
    outs = {name: [] for name in ("kp", "vp", "pp", "ks", "vs", "ps")}
    for l in range(depth):
        g_mix = norm_mix[l].reshape(1, d)
        g_ffn = norm_ffn[l].reshape(1, d)
        w_in_l = w_in[l].astype(BF16)
        w_pool_l = w_pool[l].astype(BF16)
        scale_l = pool_scale[l].reshape(1, POOL_WIDTH)
        last = l == depth - 1

        hist = jnp.concatenate([jnp.zeros((db, POOL_WIDTH), F32), to_tok_major(state_pool[l])], axis=0)
        qs, ks, vs, _, _, pool_s, utail_s = _in_proj(
            xs, g_mix, w_in_l, hist, w_pool_l, scale_l, tm=m_sample, tiles_per_seq=1, shift=db,
            n_hist=POOL_STATE, tail_rows=m_sample)
        qs, ks, vs, us = (to_seq_major(a) for a in (qs, ks, vs, utail_s[0]))

        tiles_per_seq = seq // tm_in
        q, k, v, kt, vt, pool, utail, attn_s = _in_proj(
            xp, g_mix, w_in_l, zero_hist, w_pool_l, scale_l, tm=tm_in, tiles_per_seq=tiles_per_seq,
            shift=1, n_hist=0, tail_rows=POOL_HIST,
            sample=(qs, ks, vs, sample_bias, sample_cnt, cache_kt, cache_vt), layer=l)
        attn, w_out_l, w_gate_l, w_up_l, w_down_l = _attn_prompt(
            q.reshape(b, seq, -1), k.reshape(b, seq, -1), v.reshape(b, seq, -1), prompt_bias,
            w_out, w_gate, w_up, w_down, layer=l)
        xp = _out_proj(xp, attn.reshape(b * seq, -1), pool, w_out_l, tm=2 * tm_prompt)
        xp = _ffn(xp, g_ffn, w_gate_l, w_up_l, w_down_l, gf, tm=2 * tm_prompt, tf=512, final_norm=last)
        win = min(MAX_WINDOW, seq)
        outs["kp"].append(kt.transpose(0, 3, 1, 2)[:, -win:])
        outs["vp"].append(vt.transpose(0, 3, 1, 2)[:, -win:])
        outs["pp"].append(utail.reshape(b, tiles_per_seq, POOL_HIST, -1)[:, -1, -POOL_STATE:])

        xs = _out_proj(xs, to_tok_major(attn_s), pool_s, w_out_l, tm=m_sample)
        xs = _ffn(xs, g_ffn, w_gate_l, w_up_l, w_down_l, gf, tm=m_sample, tf=512, final_norm=last)
        outs["ks"].append(ks.reshape(db, t_new, N_HEADS, HEAD_DIM))
        outs["vs"].append(vs.reshape(db, t_new, N_HEADS, HEAD_DIM))
        outs["ps"].append(jnp.concatenate([state_pool[l], us], axis=1)[:, -POOL_STATE:])

    y_prompt = xp.reshape(b, seq, d)
    y_sample = to_seq_major(xs)
    return (y_prompt, y_sample, jnp.stack(outs["kp"]), jnp.stack(outs["vp"]), jnp.stack(outs["pp"]),
            jnp.stack(outs["ks"]), jnp.stack(outs["vs"]), jnp.stack(outs["ps"]))
```

```python
import functools
import math

import numpy as np

import jax
import jax.numpy as jnp
from jax import lax
from jax.experimental import pallas as pl
from jax.experimental.pallas import tpu as pltpu

HEAD_DIM = 64
N_HEADS = 16
ATTN_WIDTH = N_HEADS * HEAD_DIM
DILATED_CONFIGS = ((128, 1), (512, 4), (2048, 16))
MAX_WINDOW = 2048
BLOCK = 128
POOL_WINDOWS = (2, 4, 8, 16)
POOL_GROUP_DIM = 256
POOL_WIDTH = len(POOL_WINDOWS) * POOL_GROUP_DIM
POOL_STATE = max(POOL_WINDOWS) - 1
POOL_HIST = max(POOL_WINDOWS)
NUM_BUCKETS = 32
EPS = 1e-6
SCALE = HEAD_DIM ** -0.5
LOG2E = math.log2(math.e)

LANES = 128
HEADS_PER_TILE = LANES // HEAD_DIM
N_HEAD_TILES = N_HEADS // HEADS_PER_TILE
VMEM_LIMIT = 56 * 1024 * 1024

F32 = jnp.float32
BF16 = jnp.bfloat16


def _params(semantics):
    return pltpu.CompilerParams(dimension_semantics=semantics, vmem_limit_bytes=VMEM_LIMIT)


def _resident(shape, index_map):
    return pl.BlockSpec(shape, index_map, pipeline_mode=pl.Buffered(1))


def _rmsnorm_rows(xf, g):
    return xf * lax.rsqrt(jnp.mean(xf * xf, axis=-1, keepdims=True) + EPS) * g


def _in_proj_kernel(x_ref, g_ref, w_ref, hist_ref, wp_ref, ps_ref, *rest,
                    tm, tn, tiles_per_seq, shift, n_hist, side_heads):
    if side_heads:
        side_in, rest = rest[:7], rest[7:]
        (q_ref, k_ref, v_ref, kt_ref, vt_ref, pool_ref, utail_ref, so_ref,
         h_ref, ucat_ref, pooled_ref, ktail_ref, vtail_ref) = rest
        side = _SampleAttnRefs(*side_in, so_ref, ktail_ref, vtail_ref)
    else:
        q_ref, k_ref, v_ref, kt_ref, vt_ref, pool_ref, utail_ref, h_ref, ucat_ref, pooled_ref = rest
    seq_tile = pl.program_id(0) % tiles_per_seq
    hist_rows = POOL_HIST * shift
    tail_rows = utail_ref.shape[0]

    @pl.when(seq_tile == 0)
    def _():
        ucat_ref[0:hist_rows, :] = hist_ref[...]

    if side_heads:
        pairs_per_step = side_heads // HEADS_PER_TILE
        scores = _sample_attn_scores(side, (pl.program_id(0) % (N_HEADS // side_heads)) * pairs_per_step)
    h_ref[...] = _rmsnorm_rows(x_ref[...], g_ref[...]).astype(BF16)
    per_out = ATTN_WIDTH // tn
    heads_per_chunk = tn // HEAD_DIM
    for c in list(range(3 * per_out, 4 * per_out)) + list(range(3 * per_out)):
        res = jnp.dot(h_ref[...], w_ref[:, c * tn:(c + 1) * tn], preferred_element_type=F32)
        which, part = divmod(c, per_out)
        cs = slice(part * tn, (part + 1) * tn)
        if which == 0:
            q_ref[:, cs] = res * (SCALE * LOG2E)
        elif which == 3:
            ucat_ref[hist_rows:hist_rows + tm, cs] = res
        else:
            nat, tr = ((k_ref, kt_ref), (v_ref, vt_ref))[which - 1]
            nat[:, cs] = res
            heads = slice(part * heads_per_chunk, (part + 1) * heads_per_chunk)
            tr[heads] = res.T.reshape(heads_per_chunk, HEAD_DIM, tm)

    utail_ref[...] = ucat_ref[hist_rows + tm - tail_rows:hist_rows + tm, :]

    chunk = min(128, tm)
    for r0 in range(0, tm, chunk):
        row = lax.broadcasted_iota(jnp.int32, (chunk, 1), 0)
        tok = (seq_tile * tm + r0 + row) // shift
        for g, w in enumerate(POOL_WINDOWS):
            cs = slice(g * POOL_GROUP_DIM, (g + 1) * POOL_GROUP_DIM)
            u0 = ucat_ref[hist_rows + r0:hist_rows + r0 + chunk, cs]
            acc = u0
            for j in range(1, w):
                lo = hist_rows + r0 - j * shift
                acc = acc + ucat_ref[lo:lo + chunk, cs]
            cnt = jnp.minimum(tok + (n_hist + 1), w).astype(F32)
            pooled_ref[r0:r0 + chunk, cs] = (acc / cnt - u0).astype(BF16)

    for g in range(len(POOL_WINDOWS)):
        cs = slice(g * POOL_GROUP_DIM, (g + 1) * POOL_GROUP_DIM)
        y = jnp.dot(pooled_ref[:, cs], wp_ref[g], preferred_element_type=F32) * ps_ref[:, cs]
        pool_ref[:, cs] = y.astype(BF16)

    if tiles_per_seq > 1:
        ucat_ref[0:hist_rows, :] = ucat_ref[tm:tm + hist_rows, :]

    if side_heads:
        _sample_attn_values(side, scores)


def _in_proj(x, g, w_in, hist, w_pool, pool_scale, *, tm, tiles_per_seq, shift, n_hist, tail_rows,
             sample=None, layer=0):
    m, d = x.shape
    in_cols = w_in.shape[1]
    hist_rows = POOL_HIST * shift
    n_tiles = m // tm
    assert m % tm == 0 and n_tiles % tiles_per_seq == 0 and hist.shape == (hist_rows, POOL_WIDTH)
    assert tiles_per_seq == 1 or tm >= hist_rows
    row_block = lambda width: pl.BlockSpec((tm, width), lambda i: (i, 0))
    tr_block = pl.BlockSpec((None, N_HEADS, HEAD_DIM, tm),
                            lambda i: (i // tiles_per_seq, 0, 0, i % tiles_per_seq))
    tr_shape = jax.ShapeDtypeStruct((n_tiles // tiles_per_seq, N_HEADS, HEAD_DIM, tiles_per_seq * tm), F32)
    in_specs = [
        row_block(d),
        _resident((1, d), lambda i: (0, 0)),
        _resident((d, in_cols), lambda i: (0, 0)),
        _resident((hist_rows, POOL_WIDTH), lambda i: (0, 0)),
        _resident(w_pool.shape, lambda i: (0, 0, 0)),
        _resident((1, POOL_WIDTH), lambda i: (0, 0)),
    ]
    out_specs = [row_block(ATTN_WIDTH)] * 3 + [tr_block] * 2 + [
        row_block(POOL_WIDTH), pl.BlockSpec((None, tail_rows, POOL_WIDTH), lambda i: (i, 0, 0))]
    out_shape = [jax.ShapeDtypeStruct((m, ATTN_WIDTH), F32)] * 3 + [tr_shape] * 2 + [
        jax.ShapeDtypeStruct((m, POOL_WIDTH), BF16),
        jax.ShapeDtypeStruct((n_tiles, tail_rows, POOL_WIDTH), F32)]
    scratch = [
        pltpu.VMEM((tm, d), BF16),
        pltpu.VMEM((hist_rows + tm, POOL_WIDTH), F32),
        pltpu.VMEM((tm, POOL_WIDTH), BF16),
    ]
    operands = [x, g, w_in, hist, w_pool, pool_scale]
    side_heads = 0
    if sample is not None:
        sq, skn, svn, sbias, scnt, ckt, cvt = sample
        db, t_new, _ = sq.shape
        wc = ckt.shape[-1]
        steps_per_seq = n_tiles // db
        assert n_tiles == db * steps_per_seq and N_HEADS % steps_per_seq == 0
        side_heads = N_HEADS // steps_per_seq
        width = side_heads * HEAD_DIM
        assert side_heads % HEADS_PER_TILE == 0 and width % LANES == 0 and HEADS_PER_TILE * t_new == 8
        assert sbias.shape == (N_HEAD_TILES, HEADS_PER_TILE * t_new, wc + BLOCK)
        assert ckt.shape[1:] == (db, N_HEADS, HEAD_DIM, wc)
        new_blk = pl.BlockSpec((None, t_new, width), lambda i: (i // steps_per_seq, 0, i % steps_per_seq))
        cache_blk = pl.BlockSpec((None, None, side_heads, HEAD_DIM, wc),
                                 lambda i: (layer, i // steps_per_seq, i % steps_per_seq, 0, 0))
        in_specs += [new_blk] * 3 + [_resident(sbias.shape, lambda i: (0, 0, 0)),
                                     _resident(scnt.shape, lambda i: (0, 0)), cache_blk, cache_blk]
        out_specs.append(new_blk)
        out_shape.append(jax.ShapeDtypeStruct(sq.shape, F32))
        scratch += [pltpu.VMEM((BLOCK, width), F32)] * 2
        operands += [sq, skn, svn, sbias, scnt, ckt, cvt]
    kern = functools.partial(_in_proj_kernel, tm=tm, tn=512, tiles_per_seq=tiles_per_seq,
                             shift=shift, n_hist=n_hist, side_heads=side_heads)
    return pl.pallas_call(
        kern,
        grid=(n_tiles,),
        in_specs=in_specs,
        out_specs=out_specs,
        out_shape=out_shape,
        scratch_shapes=scratch,
        compiler_params=_params(("arbitrary",)),
        name="in_proj",
    )(*operands)


def _attn_prompt_kernel(q_ref, k_ref, v_ref, bvec_ref, wo_ref, wg_ref, wu_ref, wd_ref,
                        o_ref, wo_bf_ref, wg_bf_ref, wu_bf_ref, wd_bf_ref, bias_ref,
                        q4_ref, k4_ref, v4_ref, m4_ref, l4_ref, acc4_ref, m_ref, l_ref, acc_ref, *, seq):
    step = pl.program_id(0) * pl.num_programs(1) + pl.program_id(1)
    for src, dst in ((wo_ref, wo_bf_ref), (wg_ref, wg_bf_ref), (wu_ref, wu_bf_ref)):
        dst[...] = src[...].astype(BF16)

    @pl.when(step % 2 == 0)
    def _():
        wd_bf_ref[...] = wd_ref[...].astype(BF16)

    @pl.when(pl.program_id(1) == 0)
    def _():
        for c in range(len(DILATED_CONFIGS)):
            for hh in range(HEADS_PER_TILE):
                vec = bvec_ref[c * HEADS_PER_TILE + hh:c * HEADS_PER_TILE + hh + 1, :]
                rows = jnp.broadcast_to(vec, (BLOCK, BIAS_PERIOD))
                bias_ref[c, hh] = pltpu.roll(rows, 0, 1, stride=1, stride_axis=0)[:, :2 * BLOCK]

    is_a = lax.broadcasted_iota(jnp.int32, (BLOCK, LANES), 1) < HEAD_DIM
    quarter = seq // 4
    nat_cfg, r4_cfg, r16_cfg = range(3)
    assert DILATED_CONFIGS[r4_cfg][1] == 4 and DILATED_CONFIGS[r16_cfg][1] == 16

    def for_chunks(fn):
        for r in range(4):
            for j in range(quarter // BLOCK):
                fn(pl.ds(r + 4 * BLOCK * j, BLOCK, stride=4), pl.ds(r * quarter + BLOCK * j, BLOCK))

    def to_residue_major(nat, r4):
        for src, dst in zip(nat, r4):
            def move(nat_rows, r4_rows, src=src, dst=dst):
                dst[r4_rows, :] = src[nat_rows, :]
            for_chunks(move)

    def to_natural(r4, nat):
        for src, dst in zip(r4, nat):
            def move(nat_rows, r4_rows, src=src, dst=dst):
                dst[nat_rows, :] = src[r4_rows, :]
            for_chunks(move)

    def compute(c, srcs, qr, kr, has_prev):
        q_src, k_src, v_src = srcs
        qb = q_src[qr, :]
        kb = k_src[kr, :].astype(BF16)
        vb = v_src[kr, :].astype(BF16)
        zero = jnp.zeros_like(qb)
        qs = jnp.concatenate([jnp.where(is_a, qb, zero), jnp.where(is_a, zero, qb)], axis=0).astype(BF16)
        bias = bias_ref[c] if has_prev else bias_ref[c, :, :, BLOCK:]
        lg = lax.dot_general(qs, kb, (((1,), (1,)), ((), ())), preferred_element_type=F32)
        lg = lg + bias.reshape(HEADS_PER_TILE * BLOCK, bias.shape[-1])
        m = jnp.max(lg, axis=-1, keepdims=True)
        p = jnp.exp2(lg - m)
        s = jnp.sum(p, axis=-1, keepdims=True)
        o = jnp.dot(p.astype(BF16), vb, preferred_element_type=F32)
        return tuple(jnp.where(is_a, x[:BLOCK], x[BLOCK:]) for x in (m, s, o))

    def rescale(m_old, m_new):
        e = jnp.exp2(-jnp.abs(m_old - m_new))
        keep = m_old >= m_new
        return jnp.where(keep, 1.0, e), jnp.where(keep, e, 1.0)

    def first_visit(state, qr, m_new, s_new, o_new):
        for ref, val in zip(state, (m_new, s_new, o_new)):
            ref[qr, :] = val

    def merge(state, qr, m_new, s_new, o_new):
        m_st, l_st, acc_st = state
        m_old = m_st[qr, :]
        a, b = rescale(m_old, m_new)
        m_st[qr, :] = jnp.maximum(m_old, m_new)
        l_st[qr, :] = l_st[qr, :] * a + s_new * b
        acc_st[qr, :] = acc_st[qr, :] * a + o_new * b

    def last_visit(state, qr, m_new, s_new, o_new):
        m_st, l_st, acc_st = state
        a, b = rescale(m_st[qr, :], m_new)
        o_ref[qr, :] = ((acc_st[qr, :] * a + o_new * b) / (l_st[qr, :] * a + s_new * b)).astype(o_ref.dtype)

    def run(c, srcs, units, visit, state):
        for qr, kr, has_prev in units:
            visit(state, qr, *compute(c, srcs, qr, kr, has_prev))

    def block_units(base, n_blocks):
        return [(pl.ds(base + n * BLOCK, BLOCK),
                 pl.ds(base + (n - 1) * BLOCK, 2 * BLOCK) if n else pl.ds(base, BLOCK), n > 0)
                for n in range(n_blocks)]

    nat_src, r4_src = (q_ref, k_ref, v_ref), (q4_ref, k4_ref, v4_ref)
    nat_state, r4_state = (m_ref, l_ref, acc_ref), (m4_ref, l4_ref, acc4_ref)
    to_residue_major(nat_src, r4_src)

    units = [u for r in range(4) for u in block_units(r * quarter, quarter // BLOCK)]
    run(r4_cfg, r4_src, units, first_visit, r4_state)

    assert seq == 16 * BLOCK
    units = []
    for r16 in range(16):
        rows16 = pl.ds((r16 % 4) * quarter + r16 // 4, BLOCK, stride=4)
        units.append((rows16, rows16, False))
    run(r16_cfg, r4_src, units, merge, r4_state)

    to_natural(r4_state, nat_state)
    run(nat_cfg, nat_src, block_units(0, seq // BLOCK), last_visit, nat_state)


def _attn_prompt(q, k, v, bias_vecs, w_out, w_gate, w_up, w_down, *, layer):
    b, seq, _ = q.shape
    assert all(seq % (BLOCK * dil) == 0 for _, dil in DILATED_CONFIGS)
    steps = N_HEAD_TILES * b
    bf16_rows = 16
    blk = pl.BlockSpec((None, seq, LANES), lambda t, i: (i, 0, t))

    def slab(w, every):
        rows = w.shape[1] * every // steps
        assert w.shape[1] * every % steps == 0 and rows % bf16_rows == 0
        return pl.BlockSpec((None, rows, w.shape[2]), lambda t, i: (layer, (t * b + i) // every, 0))

    def slab_out(w, every):
        rows = w.shape[1] * every // steps
        return pl.BlockSpec((rows, w.shape[2]), lambda t, i: ((t * b + i) // every, 0))

    weights = ((w_out, 1), (w_gate, 1), (w_up, 1), (w_down, 2))
    return pl.pallas_call(
        functools.partial(_attn_prompt_kernel, seq=seq),
        grid=(N_HEAD_TILES, b),
        in_specs=[blk, blk, blk,
                  pl.BlockSpec((None,) + bias_vecs.shape[1:], lambda t, i: (t, 0, 0))]
        + [slab(w, every) for w, every in weights],
        out_specs=[blk] + [slab_out(w, every) for w, every in weights],
        out_shape=[jax.ShapeDtypeStruct((b, seq, ATTN_WIDTH), BF16)]
        + [jax.ShapeDtypeStruct(w.shape[1:], BF16) for w, _ in weights],
        scratch_shapes=[pltpu.VMEM((len(DILATED_CONFIGS), HEADS_PER_TILE, BLOCK, 2 * BLOCK), F32)]
        + [pltpu.VMEM((seq, LANES), F32)] * 9,
        compiler_params=_params(("arbitrary", "arbitrary")),
        name="attn_prompt",
    )(q, k, v, bias_vecs, *(w for w, _ in weights))


class _SampleAttnRefs:
    def __init__(self, q, kn, vn, bias, cnt, kt, vt, o, ktail, vtail):
        self.q, self.kn, self.vn, self.bias, self.cnt = q, kn, vn, bias, cnt
        self.kt, self.vt, self.o, self.ktail, self.vtail = kt, vt, o, ktail, vtail


_CONTRACT_LAST = (((1,), (1,)), ((), ()))


def _sample_attn_scores(r, first_pair):
    t_new, wc = r.q.shape[0], r.kt.shape[-1]
    r.ktail[t_new:, :] = jnp.zeros((BLOCK - t_new, r.ktail.shape[1]), F32)
    r.vtail[t_new:, :] = jnp.zeros((BLOCK - t_new, r.vtail.shape[1]), F32)
    r.ktail[0:t_new, :] = r.kn[...]
    r.vtail[0:t_new, :] = r.vn[...]
    q = r.q[...]
    first = lax.broadcasted_iota(jnp.int32, (HEADS_PER_TILE * t_new, 1), 0) < t_new
    cnt_cache, cnt_new = r.cnt[:, :wc], r.cnt[:, wc:]
    out = []
    for j in range(r.kt.shape[0] // HEADS_PER_TILE):
        pair = (HEADS_PER_TILE * j, HEADS_PER_TILE * j + 1)
        cols = [slice(h * HEAD_DIM, (h + 1) * HEAD_DIM) for h in pair]
        lhs = jnp.concatenate([q[:, cs] for cs in cols], axis=0).astype(BF16)
        own = lambda a, b: jnp.where(first, a, b)
        lc = own(*(jnp.dot(lhs, r.kt[h].astype(BF16), preferred_element_type=F32) for h in pair))
        ln = own(*(lax.dot_general(lhs, r.ktail[:, cs].astype(BF16), _CONTRACT_LAST,
                                   preferred_element_type=F32) for cs in cols))
        bias = r.bias[first_pair + j]
        lc = lc + bias[:, :wc]
        ln = ln + bias[:, wc:]
        m = jnp.maximum(jnp.max(lc, axis=-1, keepdims=True), jnp.max(ln, axis=-1, keepdims=True))
        pc = cnt_cache * jnp.exp2(lc - m)
        pn = cnt_new * jnp.exp2(ln - m)
        s = jnp.sum(pc, axis=-1, keepdims=True) + jnp.sum(pn, axis=-1, keepdims=True)
        out.append((pair, cols, pc.astype(BF16), pn.astype(BF16), s))
    return out


def _sample_attn_values(r, scores):
    t_new = r.q.shape[0]
    for pair, cols, pc, pn, s in scores:
        for h, cs, rows in zip(pair, cols, (slice(0, t_new), slice(t_new, 2 * t_new))):
            o = (lax.dot_general(pc, r.vt[h].astype(BF16), _CONTRACT_LAST, preferred_element_type=F32)
                 + jnp.dot(pn, r.vtail[:, cs].astype(BF16), preferred_element_type=F32)) / s
            r.o[:, cs] = o[rows]


def _out_proj_kernel(x_ref, a_ref, p_ref, w_ref, o_ref, *, tn):
    a = a_ref[...].astype(BF16)
    p = p_ref[...]
    for c in range(o_ref.shape[1] // tn):
        cs = slice(c * tn, (c + 1) * tn)
        mixed = (jnp.dot(a, w_ref[0:ATTN_WIDTH, cs], preferred_element_type=F32)
                 + jnp.dot(p, w_ref[ATTN_WIDTH:, cs], preferred_element_type=F32))
        o_ref[:, cs] = x_ref[:, cs] + mixed


def _out_proj(x, attn, pool, w_out, *, tm):
    m, d = x.shape
    row_block = lambda width: pl.BlockSpec((tm, width), lambda i: (i, 0))
    return pl.pallas_call(
        functools.partial(_out_proj_kernel, tn=512),
        grid=(m // tm,),
        in_specs=[row_block(d), row_block(ATTN_WIDTH), row_block(POOL_WIDTH),
                  _resident(w_out.shape, lambda i: (0, 0))],
        out_specs=row_block(d),
        out_shape=jax.ShapeDtypeStruct((m, d), F32),
        compiler_params=_params(("arbitrary",)),
        name="out_proj",
    )(x, attn, pool, w_out)


def _ffn_kernel(x_ref, g_ref, wg_ref, wu_ref, wd_ref, gf_ref, o_ref, h_ref, *, final_norm):
    j = pl.program_id(1)

    @pl.when(j == 0)
    def _():
        xf = x_ref[...]
        h_ref[...] = _rmsnorm_rows(xf, g_ref[...]).astype(BF16)
        o_ref[...] = xf

    h = h_ref[...]
    half = wg_ref.shape[1] // 2
    partial = None
    for c in range(2):
        cs = slice(c * half, (c + 1) * half)
        gate = jnp.dot(h, wg_ref[:, cs], preferred_element_type=F32)
        up = jnp.dot(h, wu_ref[:, cs], preferred_element_type=F32)
        act = (gate * jax.nn.sigmoid(gate) * up).astype(BF16)
        down = jnp.dot(act, wd_ref[cs, :], preferred_element_type=F32)
        partial = down if partial is None else partial + down
    o_ref[...] += partial

    if final_norm:
        @pl.when(j == pl.num_programs(1) - 1)
        def _():
            o_ref[...] = _rmsnorm_rows(o_ref[...], gf_ref[...])


def _ffn(x, g, w_gate, w_up, w_down, g_final, *, tm, tf, final_norm):
    m, d = x.shape
    f = w_gate.shape[1]
    assert m % tm == 0 and f % tf == 0
    return pl.pallas_call(
        functools.partial(_ffn_kernel, final_norm=final_norm),
        grid=(m // tm, f // tf),
        in_specs=[pl.BlockSpec((tm, d), lambda i, j: (i, 0)),
                  _resident((1, d), lambda i, j: (0, 0)),
                  pl.BlockSpec((d, tf), lambda i, j: (0, j)),
                  pl.BlockSpec((d, tf), lambda i, j: (0, j)),
                  pl.BlockSpec((tf, d), lambda i, j: (j, 0)),
                  _resident((1, d), lambda i, j: (0, 0))],
        out_specs=pl.BlockSpec((tm, d), lambda i, j: (i, 0)),
        out_shape=jax.ShapeDtypeStruct((m, d), F32),
        scratch_shapes=[pltpu.VMEM((tm, d), BF16)],
        compiler_params=_params(("arbitrary", "arbitrary")),
        name="ffn",
    )(x, g, w_gate, w_up, w_down, g_final)


def _t5_bucket(dist):
    max_exact = NUM_BUCKETS // 2
    df = jnp.maximum(dist, 1).astype(F32)
    large = max_exact + (jnp.log(df / max_exact) / math.log(MAX_WINDOW / max_exact)
                         * (NUM_BUCKETS - max_exact)).astype(jnp.int32)
    large = jnp.minimum(large, NUM_BUCKETS - 1)
    return jnp.where(dist < max_exact, dist, large)


BIAS_PERIOD = 3 * BLOCK


def _prompt_bias_vectors(rel_bias):
    vecs = []
    for window, dil in DILATED_CONFIGS:
        sub_w = window // dil
        assert sub_w <= BLOCK
        bias_sub = rel_bias[_t5_bucket(dil * jnp.arange(sub_w + 1))].T.astype(F32) * LOG2E
        vec = jnp.full((N_HEADS, BIAS_PERIOD), -jnp.inf, F32)
        vecs.append(vec.at[:, BLOCK - sub_w:BLOCK + 1].set(bias_sub[:, ::-1]))
    vecs = jnp.stack(vecs, axis=1)
    vecs = vecs.reshape(N_HEAD_TILES, HEADS_PER_TILE, len(DILATED_CONFIGS), BIAS_PERIOD)
    return vecs.transpose(0, 2, 1, 3).reshape(N_HEAD_TILES, -1, BIAS_PERIOD)


def _sample_tables(rel_bias, wc, t_new):
    nk = wc + BLOCK
    tail = np.where(np.arange(BLOCK) < t_new, wc + np.arange(BLOCK), -1)
    key_pos = np.concatenate([np.arange(wc), tail])
    dist = wc + np.arange(t_new)[:, None] - key_pos[None, :]
    listed = (key_pos >= 0)[None, :] & (dist >= 0)
    cnt = sum((listed & (dist <= w) & (dist % d == 0)).astype(np.float32) for w, d in DILATED_CONFIGS)
    max_dist = wc + t_new - 1
    by_dist = rel_bias[_t5_bucket(jnp.arange(max_dist + 1))].T.astype(F32) * LOG2E
    rev = jnp.pad(by_dist[:, ::-1], ((0, 0), (0, nk)))
    bias = jnp.stack([rev[:, t_new - 1 - t:t_new - 1 - t + nk] for t in range(t_new)], axis=1)
    bias = jnp.where(jnp.asarray(cnt > 0)[None], bias, -jnp.inf)
    bias = bias.reshape(N_HEAD_TILES, HEADS_PER_TILE * t_new, nk)
    return bias, jnp.asarray(np.tile(cnt, (HEADS_PER_TILE, 1)))


def kernel(x_prompt, x_sample, cache_k, cache_v, state_pool, rel_bias, norm_mix, w_in, w_pool,
           pool_scale, w_out, norm_ffn, w_gate, w_up, w_down, norm_final):
    b, seq, d = x_prompt.shape
    db, t_new, _ = x_sample.shape
    depth = w_in.shape[0]
    wc = cache_k.shape[2]
    assert depth >= 1 and wc == MAX_WINDOW and seq >= MAX_WINDOW and t_new <= POOL_STATE

    tm_prompt = 512
    tm_in = b * seq // (2 * db)
    m_sample = db * t_new
    gf = norm_final.reshape(1, d)
    prompt_bias = _prompt_bias_vectors(rel_bias)
    sample_bias, sample_cnt = _sample_tables(rel_bias, wc, t_new)
    cache_kt, cache_vt = (c.transpose(0, 1, 3, 4, 2) for c in (cache_k, cache_v))
    zero_hist = jnp.zeros((POOL_HIST, POOL_WIDTH), F32)

    def to_tok_major(a):
        return a.transpose(1, 0, 2).reshape(a.shape[1] * db, a.shape[2])

    def to_seq_major(a):
        return a.reshape(a.shape[0] // db, db, a.shape[1]).transpose(1, 0, 2)

    xp = x_prompt.reshape(b * seq, d)
    xs = to_tok_major(x_sample)
    outs = {name: [] for name in ("kp", "vp", "pp", "ks", "vs", "ps")}
    for l in range(depth):
        g_mix = norm_mix[l].reshape(1, d)
        g_ffn = norm_ffn[l].reshape(1, d)
        w_in_l = w_in[l].astype(BF16)
        w_pool_l = w_pool[l].astype(BF16)
        scale_l = pool_scale[l].reshape(1, POOL_WIDTH)
        last = l == depth - 1

        hist = jnp.concatenate([jnp.zeros((db, POOL_WIDTH), F32), to_tok_major(state_pool[l])], axis=0)
        qs, ks, vs, _, _, pool_s, utail_s = _in_proj(
            xs, g_mix, w_in_l, hist, w_pool_l, scale_l, tm=m_sample, tiles_per_seq=1, shift=db,
            n_hist=POOL_STATE, tail_rows=m_sample)
        qs, ks, vs, us = (to_seq_major(a) for a in (qs, ks, vs, utail_s[0]))

        tiles_per_seq = seq // tm_in
        q, k, v, kt, vt, pool, utail, attn_s = _in_proj(
            xp, g_mix, w_in_l, zero_hist, w_pool_l, scale_l, tm=tm_in, tiles_per_seq=tiles_per_seq,
            shift=1, n_hist=0, tail_rows=POOL_HIST,
            sample=(qs, ks, vs, sample_bias, sample_cnt, cache_kt, cache_vt), layer=l)
        attn, w_out_l, w_gate_l, w_up_l, w_down_l = _attn_prompt(
            q.reshape(b, seq, -1), k.reshape(b, seq, -1), v.reshape(b, seq, -1), prompt_bias,
            w_out, w_gate, w_up, w_down, layer=l)
        xp = _out_proj(xp, attn.reshape(b * seq, -1), pool, w_out_l, tm=2 * tm_prompt)
        xp = _ffn(xp, g_ffn, w_gate_l, w_up_l, w_down_l, gf, tm=2 * tm_prompt, tf=512, final_norm=last)
        win = min(MAX_WINDOW, seq)
        outs["kp"].append(kt.transpose(0, 3, 1, 2)[:, -win:])
        outs["vp"].append(vt.transpose(0, 3, 1, 2)[:, -win:])
        outs["pp"].append(utail.reshape(b, tiles_per_seq, POOL_HIST, -1)[:, -1, -POOL_STATE:])

        xs = _out_proj(xs, to_tok_major(attn_s), pool_s, w_out_l, tm=m_sample)
        xs = _ffn(xs, g_ffn, w_gate_l, w_up_l, w_down_l, gf, tm=m_sample, tf=512, final_norm=last)
        outs["ks"].append(ks.reshape(db, t_new, N_HEADS, HEAD_DIM))
        outs["vs"].append(vs.reshape(db, t_new, N_HEADS, HEAD_DIM))
        outs["ps"].append(jnp.concatenate([state_pool[l], us], axis=1)[:, -POOL_STATE:])

    y_prompt = xp.reshape(b, seq, d)
    y_sample = to_seq_major(xs)
    return (y_prompt, y_sample, jnp.stack(outs["kp"]), jnp.stack(outs["vp"]), jnp.stack(outs["pp"]),
            jnp.stack(outs["ks"]), jnp.stack(outs["vs"]), jnp.stack(outs["ps"]))
```

```python
import functools
import math

import numpy as np

import jax
import jax.numpy as jnp
from jax import lax
from jax.experimental import pallas as pl
from jax.experimental.pallas import tpu as pltpu

HEAD_DIM = 64
N_HEADS = 16
ATTN_WIDTH = N_HEADS * HEAD_DIM
DILATED_CONFIGS = ((128, 1), (512, 4), (2048, 16))
MAX_WINDOW = 2048
BLOCK = 128
POOL_WINDOWS = (2, 4, 8, 16)
POOL_GROUP_DIM = 256
POOL_WIDTH = len(POOL_WINDOWS) * POOL_GROUP_DIM
POOL_STATE = max(POOL_WINDOWS) - 1
POOL_HIST = max(POOL_WINDOWS)
NUM_BUCKETS = 32
EPS = 1e-6
SCALE = HEAD_DIM ** -0.5
LOG2E = math.log2(math.e)

LANES = 128
HEADS_PER_TILE = LANES // HEAD_DIM
N_HEAD_TILES = N_HEADS // HEADS_PER_TILE
VMEM_LIMIT = 56 * 1024 * 1024

F32 = jnp.float32
BF16 = jnp.bfloat16


def _params(semantics):
    return pltpu.CompilerParams(dimension_semantics=semantics, vmem_limit_bytes=VMEM_LIMIT)


def _resident(shape, index_map):
    return pl.BlockSpec(shape, index_map, pipeline_mode=pl.Buffered(1))


def _rmsnorm_rows(xf, g):
    return xf * lax.rsqrt(jnp.mean(xf * xf, axis=-1, keepdims=True) + EPS) * g


def _in_proj_kernel(x_ref, g_ref, w_ref, hist_ref, wp_ref, ps_ref, *rest,
                    tm, tn, tiles_per_seq, shift, n_hist, side_heads):
    if side_heads:
        side_in, rest = rest[:7], rest[7:]
        (q_ref, k_ref, v_ref, kt_ref, vt_ref, pool_ref, utail_ref, so_ref,
         h_ref, ucat_ref, pooled_ref, ktail_ref, vtail_ref) = rest
        side = _SampleAttnRefs(*side_in, so_ref, ktail_ref, vtail_ref)
    else:
        q_ref, k_ref, v_ref, kt_ref, vt_ref, pool_ref, utail_ref, h_ref, ucat_ref, pooled_ref = rest
    seq_tile = pl.program_id(0) % tiles_per_seq
    hist_rows = POOL_HIST * shift
    tail_rows = utail_ref.shape[0]

    @pl.when(seq_tile == 0)
    def _():
        ucat_ref[0:hist_rows, :] = hist_ref[...]

    if side_heads:
        pairs_per_step = side_heads // HEADS_PER_TILE
        scores = _sample_attn_scores(side, (pl.program_id(0) % (N_HEADS // side_heads)) * pairs_per_step)
    h_ref[...] = _rmsnorm_rows(x_ref[...], g_ref[...]).astype(BF16)
    per_out = ATTN_WIDTH // tn
    heads_per_chunk = tn // HEAD_DIM
    for n_done, c in enumerate(list(range(3 * per_out, 4 * per_out)) + list(range(3 * per_out))):
        if side_heads and n_done == 3 * per_out:
            _sample_attn_values(side, scores)
        res = jnp.dot(h_ref[...], w_ref[:, c * tn:(c + 1) * tn], preferred_element_type=F32)
        which, part = divmod(c, per_out)
        cs = slice(part * tn, (part + 1) * tn)
        if which == 0:
            q_ref[:, cs] = res * (SCALE * LOG2E)
        elif which == 3:
            ucat_ref[hist_rows:hist_rows + tm, cs] = res
        else:
            nat, tr = ((k_ref, kt_ref), (v_ref, vt_ref))[which - 1]
            nat[:, cs] = res
            heads = slice(part * heads_per_chunk, (part + 1) * heads_per_chunk)
            tr[heads] = res.T.reshape(heads_per_chunk, HEAD_DIM, tm)

    utail_ref[...] = ucat_ref[hist_rows + tm - tail_rows:hist_rows + tm, :]

    chunk = min(128, tm)
    for r0 in range(0, tm, chunk):
        row = lax.broadcasted_iota(jnp.int32, (chunk, 1), 0)
        tok = (seq_tile * tm + r0 + row) // shift
        for g, w in enumerate(POOL_WINDOWS):
            cs = slice(g * POOL_GROUP_DIM, (g + 1) * POOL_GROUP_DIM)
            u0 = ucat_ref[hist_rows + r0:hist_rows + r0 + chunk, cs]
            acc = u0
            for j in range(1, w):
                lo = hist_rows + r0 - j * shift
                acc = acc + ucat_ref[lo:lo + chunk, cs]
            cnt = jnp.minimum(tok + (n_hist + 1), w).astype(F32)
            pooled_ref[r0:r0 + chunk, cs] = (acc / cnt - u0).astype(BF16)

    for g in range(len(POOL_WINDOWS)):
        cs = slice(g * POOL_GROUP_DIM, (g + 1) * POOL_GROUP_DIM)
        y = jnp.dot(pooled_ref[:, cs], wp_ref[g], preferred_element_type=F32) * ps_ref[:, cs]
        pool_ref[:, cs] = y.astype(BF16)

    if tiles_per_seq > 1:
        ucat_ref[0:hist_rows, :] = ucat_ref[tm:tm + hist_rows, :]


def _in_proj(x, g, w_in, hist, w_pool, pool_scale, *, tm, tiles_per_seq, shift, n_hist, tail_rows,
             sample=None, layer=0):
    m, d = x.shape
    in_cols = w_in.shape[1]
    hist_rows = POOL_HIST * shift
    n_tiles = m // tm
    assert m % tm == 0 and n_tiles % tiles_per_seq == 0 and hist.shape == (hist_rows, POOL_WIDTH)
    assert tiles_per_seq == 1 or tm >= hist_rows
    row_block = lambda width: pl.BlockSpec((tm, width), lambda i: (i, 0))
    tr_block = pl.BlockSpec((None, N_HEADS, HEAD_DIM, tm),
                            lambda i: (i // tiles_per_seq, 0, 0, i % tiles_per_seq))
    tr_shape = jax.ShapeDtypeStruct((n_tiles // tiles_per_seq, N_HEADS, HEAD_DIM, tiles_per_seq * tm), F32)
    in_specs = [
        row_block(d),
        _resident((1, d), lambda i: (0, 0)),
        _resident((d, in_cols), lambda i: (0, 0)),
        _resident((hist_rows, POOL_WIDTH), lambda i: (0, 0)),
        _resident(w_pool.shape, lambda i: (0, 0, 0)),
        _resident((1, POOL_WIDTH), lambda i: (0, 0)),
    ]
    out_specs = [row_block(ATTN_WIDTH)] * 3 + [tr_block] * 2 + [
        row_block(POOL_WIDTH), pl.BlockSpec((None, tail_rows, POOL_WIDTH), lambda i: (i, 0, 0))]
    out_shape = [jax.ShapeDtypeStruct((m, ATTN_WIDTH), F32)] * 3 + [tr_shape] * 2 + [
        jax.ShapeDtypeStruct((m, POOL_WIDTH), BF16),
        jax.ShapeDtypeStruct((n_tiles, tail_rows, POOL_WIDTH), F32)]
    scratch = [
        pltpu.VMEM((tm, d), BF16),
        pltpu.VMEM((hist_rows + tm, POOL_WIDTH), F32),
        pltpu.VMEM((tm, POOL_WIDTH), BF16),
    ]
    operands = [x, g, w_in, hist, w_pool, pool_scale]
    side_heads = 0
    if sample is not None:
        sq, skn, svn, sbias, scnt, ckt, cvt = sample
        db, t_new, _ = sq.shape
        wc = ckt.shape[-1]
        steps_per_seq = n_tiles // db
        assert n_tiles == db * steps_per_seq and N_HEADS % steps_per_seq == 0
        side_heads = N_HEADS // steps_per_seq
        width = side_heads * HEAD_DIM
        assert side_heads % HEADS_PER_TILE == 0 and width % LANES == 0 and HEADS_PER_TILE * t_new == 8
        assert sbias.shape == (N_HEAD_TILES, HEADS_PER_TILE * t_new, wc + BLOCK)
        assert ckt.shape[1:] == (db, N_HEADS, HEAD_DIM, wc)
        new_blk = pl.BlockSpec((None, t_new, width), lambda i: (i // steps_per_seq, 0, i % steps_per_seq))
        cache_blk = pl.BlockSpec((None, None, side_heads, HEAD_DIM, wc),
                                 lambda i: (layer, i // steps_per_seq, i % steps_per_seq, 0, 0))
        in_specs += [new_blk] * 3 + [_resident(sbias.shape, lambda i: (0, 0, 0)),
                                     _resident(scnt.shape, lambda i: (0, 0)), cache_blk, cache_blk]
        out_specs.append(new_blk)
        out_shape.append(jax.ShapeDtypeStruct(sq.shape, F32))
        scratch += [pltpu.VMEM((BLOCK, width), F32)] * 2
        operands += [sq, skn, svn, sbias, scnt, ckt, cvt]
    kern = functools.partial(_in_proj_kernel, tm=tm, tn=512, tiles_per_seq=tiles_per_seq,
                             shift=shift, n_hist=n_hist, side_heads=side_heads)
    return pl.pallas_call(
        kern,
        grid=(n_tiles,),
        in_specs=in_specs,
        out_specs=out_specs,
        out_shape=out_shape,
        scratch_shapes=scratch,
        compiler_params=_params(("arbitrary",)),
        name="in_proj",
    )(*operands)


def _attn_prompt_kernel(q_ref, k_ref, v_ref, bvec_ref, wo_ref, wg_ref, wu_ref, wd_ref,
                        o_ref, wo_bf_ref, wg_bf_ref, wu_bf_ref, wd_bf_ref, bias_ref,
                        q4_ref, k4_ref, v4_ref, m4_ref, l4_ref, acc4_ref, m_ref, l_ref, acc_ref, *, seq):
    step = pl.program_id(0) * pl.num_programs(1) + pl.program_id(1)
    for src, dst in ((wo_ref, wo_bf_ref), (wg_ref, wg_bf_ref), (wu_ref, wu_bf_ref)):
        dst[...] = src[...].astype(BF16)

    @pl.when(step % 2 == 0)
    def _():
        wd_bf_ref[...] = wd_ref[...].astype(BF16)

    @pl.when(pl.program_id(1) == 0)
    def _():
        for c in range(len(DILATED_CONFIGS)):
            for hh in range(HEADS_PER_TILE):
                vec = bvec_ref[c * HEADS_PER_TILE + hh:c * HEADS_PER_TILE + hh + 1, :]
                rows = jnp.broadcast_to(vec, (BLOCK, BIAS_PERIOD))
                bias_ref[c, hh] = pltpu.roll(rows, 0, 1, stride=1, stride_axis=0)[:, :2 * BLOCK]

    is_a = lax.broadcasted_iota(jnp.int32, (BLOCK, LANES), 1) < HEAD_DIM
    quarter = seq // 4
    nat_cfg, r4_cfg, r16_cfg = range(3)
    assert DILATED_CONFIGS[r4_cfg][1] == 4 and DILATED_CONFIGS[r16_cfg][1] == 16

    def for_chunks(fn):
        for r in range(4):
            for j in range(quarter // BLOCK):
                fn(pl.ds(r + 4 * BLOCK * j, BLOCK, stride=4), pl.ds(r * quarter + BLOCK * j, BLOCK))

    def to_residue_major(nat, r4):
        for src, dst in zip(nat, r4):
            def move(nat_rows, r4_rows, src=src, dst=dst):
                dst[r4_rows, :] = src[nat_rows, :]
            for_chunks(move)

    def to_natural(r4, nat):
        for src, dst in zip(r4, nat):
            def move(nat_rows, r4_rows, src=src, dst=dst):
                dst[nat_rows, :] = src[r4_rows, :]
            for_chunks(move)

    def compute(c, srcs, qr, kr, has_prev):
        q_src, k_src, v_src = srcs
        qb = q_src[qr, :]
        kb = k_src[kr, :].astype(BF16)
        vb = v_src[kr, :].astype(BF16)
        zero = jnp.zeros_like(qb)
        qs = jnp.concatenate([jnp.where(is_a, qb, zero), jnp.where(is_a, zero, qb)], axis=0).astype(BF16)
        bias = bias_ref[c] if has_prev else bias_ref[c, :, :, BLOCK:]
        lg = lax.dot_general(qs, kb, (((1,), (1,)), ((), ())), preferred_element_type=F32)
        lg = lg + bias.reshape(HEADS_PER_TILE * BLOCK, bias.shape[-1])
        m = jnp.max(lg, axis=-1, keepdims=True)
        p = jnp.exp2(lg - m)
        s = jnp.sum(p, axis=-1, keepdims=True)
        o = jnp.dot(p.astype(BF16), vb, preferred_element_type=F32)
        return tuple(jnp.where(is_a, x[:BLOCK], x[BLOCK:]) for x in (m, s, o))

    def rescale(m_old, m_new):
        e = jnp.exp2(-jnp.abs(m_old - m_new))
        keep = m_old >= m_new
        return jnp.where(keep, 1.0, e), jnp.where(keep, e, 1.0)

    def first_visit(state, qr, m_new, s_new, o_new):
        for ref, val in zip(state, (m_new, s_new, o_new)):
            ref[qr, :] = val

    def merge(state, qr, m_new, s_new, o_new):
        m_st, l_st, acc_st = state
        m_old = m_st[qr, :]
        a, b = rescale(m_old, m_new)
        m_st[qr, :] = jnp.maximum(m_old, m_new)
        l_st[qr, :] = l_st[qr, :] * a + s_new * b
        acc_st[qr, :] = acc_st[qr, :] * a + o_new * b

    def last_visit(state, qr, m_new, s_new, o_new):
        m_st, l_st, acc_st = state
        a, b = rescale(m_st[qr, :], m_new)
        o_ref[qr, :] = ((acc_st[qr, :] * a + o_new * b) / (l_st[qr, :] * a + s_new * b)).astype(o_ref.dtype)

    def run(c, srcs, units, visit, state):
        for qr, kr, has_prev in units:
            visit(state, qr, *compute(c, srcs, qr, kr, has_prev))

    def block_units(base, n_blocks):
        return [(pl.ds(base + n * BLOCK, BLOCK),
                 pl.ds(base + (n - 1) * BLOCK, 2 * BLOCK) if n else pl.ds(base, BLOCK), n > 0)
                for n in range(n_blocks)]

    nat_src, r4_src = (q_ref, k_ref, v_ref), (q4_ref, k4_ref, v4_ref)
    nat_state, r4_state = (m_ref, l_ref, acc_ref), (m4_ref, l4_ref, acc4_ref)
    to_residue_major(nat_src, r4_src)

    units = [u for r in range(4) for u in block_units(r * quarter, quarter // BLOCK)]
    run(r4_cfg, r4_src, units, first_visit, r4_state)

    assert seq == 16 * BLOCK
    units = []
    for r16 in range(16):
        rows16 = pl.ds((r16 % 4) * quarter + r16 // 4, BLOCK, stride=4)
        units.append((rows16, rows16, False))
    run(r16_cfg, r4_src, units, merge, r4_state)

    to_natural(r4_state, nat_state)
    run(nat_cfg, nat_src, block_units(0, seq // BLOCK), last_visit, nat_state)


def _attn_prompt(q, k, v, bias_vecs, w_out, w_gate, w_up, w_down, *, layer):
    b, seq, _ = q.shape
    assert all(seq % (BLOCK * dil) == 0 for _, dil in DILATED_CONFIGS)
    steps = N_HEAD_TILES * b
    bf16_rows = 16
    blk = pl.BlockSpec((None, seq, LANES), lambda t, i: (i, 0, t))

    def slab(w, every):
        rows = w.shape[1] * every // steps
        assert w.shape[1] * every % steps == 0 and rows % bf16_rows == 0
        return pl.BlockSpec((None, rows, w.shape[2]), lambda t, i: (layer, (t * b + i) // every, 0))

    def slab_out(w, every):
        rows = w.shape[1] * every // steps
        return pl.BlockSpec((rows, w.shape[2]), lambda t, i: ((t * b + i) // every, 0))

    weights = ((w_out, 1), (w_gate, 1), (w_up, 1), (w_down, 2))
    return pl.pallas_call(
        functools.partial(_attn_prompt_kernel, seq=seq),
        grid=(N_HEAD_TILES, b),
        in_specs=[blk, blk, blk,
                  pl.BlockSpec((None,) + bias_vecs.shape[1:], lambda t, i: (t, 0, 0))]
        + [slab(w, every) for w, every in weights],
        out_specs=[blk] + [slab_out(w, every) for w, every in weights],
        out_shape=[jax.ShapeDtypeStruct((b, seq, ATTN_WIDTH), BF16)]
        + [jax.ShapeDtypeStruct(w.shape[1:], BF16) for w, _ in weights],
        scratch_shapes=[pltpu.VMEM((len(DILATED_CONFIGS), HEADS_PER_TILE, BLOCK, 2 * BLOCK), F32)]
        + [pltpu.VMEM((seq, LANES), F32)] * 9,
        compiler_params=_params(("arbitrary", "arbitrary")),
        name="attn_prompt",
    )(q, k, v, bias_vecs, *(w for w, _ in weights))


class _SampleAttnRefs:
    def __init__(self, q, kn, vn, bias, cnt, kt, vt, o, ktail, vtail):
        self.q, self.kn, self.vn, self.bias, self.cnt = q, kn, vn, bias, cnt
        self.kt, self.vt, self.o, self.ktail, self.vtail = kt, vt, o, ktail, vtail


_CONTRACT_LAST = (((1,), (1,)), ((), ()))


def _sample_attn_scores(r, first_pair):
    t_new, wc = r.q.shape[0], r.kt.shape[-1]
    r.ktail[t_new:, :] = jnp.zeros((BLOCK - t_new, r.ktail.shape[1]), F32)
    r.vtail[t_new:, :] = jnp.zeros((BLOCK - t_new, r.vtail.shape[1]), F32)
    r.ktail[0:t_new, :] = r.kn[...]
    r.vtail[0:t_new, :] = r.vn[...]
    q = r.q[...]
    first = lax.broadcasted_iota(jnp.int32, (HEADS_PER_TILE * t_new, 1), 0) < t_new
    cnt_cache, cnt_new = r.cnt[:, :wc], r.cnt[:, wc:]
    out = []
    for j in range(r.kt.shape[0] // HEADS_PER_TILE):
        pair = (HEADS_PER_TILE * j, HEADS_PER_TILE * j + 1)
        cols = [slice(h * HEAD_DIM, (h + 1) * HEAD_DIM) for h in pair]
        lhs = jnp.concatenate([q[:, cs] for cs in cols], axis=0).astype(BF16)
        own = lambda a, b: jnp.where(first, a, b)
        lc = own(*(jnp.dot(lhs, r.kt[h].astype(BF16), preferred_element_type=F32) for h in pair))
        ln = own(*(lax.dot_general(lhs, r.ktail[:, cs].astype(BF16), _CONTRACT_LAST,
                                   preferred_element_type=F32) for cs in cols))
        bias = r.bias[first_pair + j]
        lc = lc + bias[:, :wc]
        ln = ln + bias[:, wc:]
        m = jnp.maximum(jnp.max(lc, axis=-1, keepdims=True), jnp.max(ln, axis=-1, keepdims=True))
        pc = cnt_cache * jnp.exp2(lc - m)
        pn = cnt_new * jnp.exp2(ln - m)
        s = jnp.sum(pc, axis=-1, keepdims=True) + jnp.sum(pn, axis=-1, keepdims=True)
        out.append((pair, cols, pc.astype(BF16), pn.astype(BF16), s))
    return out


def _sample_attn_values(r, scores):
    t_new = r.q.shape[0]
    for pair, cols, pc, pn, s in scores:
        for h, cs, rows in zip(pair, cols, (slice(0, t_new), slice(t_new, 2 * t_new))):
            o = (lax.dot_general(pc, r.vt[h].astype(BF16), _CONTRACT_LAST, preferred_element_type=F32)
                 + jnp.dot(pn, r.vtail[:, cs].astype(BF16), preferred_element_type=F32)) / s
            r.o[:, cs] = o[rows]


def _out_proj_kernel(x_ref, a_ref, p_ref, w_ref, o_ref, *, tn):
    a = a_ref[...].astype(BF16)
    p = p_ref[...]
    for c in range(o_ref.shape[1] // tn):
        cs = slice(c * tn, (c + 1) * tn)
        mixed = (jnp.dot(a, w_ref[0:ATTN_WIDTH, cs], preferred_element_type=F32)
                 + jnp.dot(p, w_ref[ATTN_WIDTH:, cs], preferred_element_type=F32))
        o_ref[:, cs] = x_ref[:, cs] + mixed


def _out_proj(x, attn, pool, w_out, *, tm):
    m, d = x.shape
    row_block = lambda width: pl.BlockSpec((tm, width), lambda i: (i, 0))
    return pl.pallas_call(
        functools.partial(_out_proj_kernel, tn=512),
        grid=(m // tm,),
        in_specs=[row_block(d), row_block(ATTN_WIDTH), row_block(POOL_WIDTH),
                  _resident(w_out.shape, lambda i: (0, 0))],
        out_specs=row_block(d),
        out_shape=jax.ShapeDtypeStruct((m, d), F32),
        compiler_params=_params(("arbitrary",)),
        name="out_proj",
    )(x, attn, pool, w_out)


def _ffn_kernel(x_ref, g_ref, wg_ref, wu_ref, wd_ref, gf_ref, o_ref, h_ref, *, final_norm):
    j = pl.program_id(1)

    @pl.when(j == 0)
    def _():
        xf = x_ref[...]
        h_ref[...] = _rmsnorm_rows(xf, g_ref[...]).astype(BF16)
        o_ref[...] = xf

    h = h_ref[...]
    half = wg_ref.shape[1] // 2
    partial = None
    for c in range(2):
        cs = slice(c * half, (c + 1) * half)
        gate = jnp.dot(h, wg_ref[:, cs], preferred_element_type=F32)
        up = jnp.dot(h, wu_ref[:, cs], preferred_element_type=F32)
        act = (gate * jax.nn.sigmoid(gate) * up).astype(BF16)
        down = jnp.dot(act, wd_ref[cs, :], preferred_element_type=F32)
        partial = down if partial is None else partial + down
    o_ref[...] += partial

    if final_norm:
        @pl.when(j == pl.num_programs(1) - 1)
        def _():
            o_ref[...] = _rmsnorm_rows(o_ref[...], gf_ref[...])


def _ffn(x, g, w_gate, w_up, w_down, g_final, *, tm, tf, final_norm):
    m, d = x.shape
    f = w_gate.shape[1]
    assert m % tm == 0 and f % tf == 0
    return pl.pallas_call(
        functools.partial(_ffn_kernel, final_norm=final_norm),
        grid=(m // tm, f // tf),
        in_specs=[pl.BlockSpec((tm, d), lambda i, j: (i, 0)),
                  _resident((1, d), lambda i, j: (0, 0)),
                  pl.BlockSpec((d, tf), lambda i, j: (0, j)),
                  pl.BlockSpec((d, tf), lambda i, j: (0, j)),
                  pl.BlockSpec((tf, d), lambda i, j: (j, 0)),
                  _resident((1, d), lambda i, j: (0, 0))],
        out_specs=pl.BlockSpec((tm, d), lambda i, j: (i, 0)),
        out_shape=jax.ShapeDtypeStruct((m, d), F32),
        scratch_shapes=[pltpu.VMEM((tm, d), BF16)],
        compiler_params=_params(("arbitrary", "arbitrary")),
        name="ffn",
    )(x, g, w_gate, w_up, w_down, g_final)


def _t5_bucket(dist):
    max_exact = NUM_BUCKETS // 2
    df = jnp.maximum(dist, 1).astype(F32)
    large = max_exact + (jnp.log(df / max_exact) / math.log(MAX_WINDOW / max_exact)
                         * (NUM_BUCKETS - max_exact)).astype(jnp.int32)
    large = jnp.minimum(large, NUM_BUCKETS - 1)
    return jnp.where(dist < max_exact, dist, large)


BIAS_PERIOD = 3 * BLOCK


def _prompt_bias_vectors(rel_bias):
    vecs = []
    for window, dil in DILATED_CONFIGS:
        sub_w = window // dil
        assert sub_w <= BLOCK
        bias_sub = rel_bias[_t5_bucket(dil * jnp.arange(sub_w + 1))].T.astype(F32) * LOG2E
        vec = jnp.full((N_HEADS, BIAS_PERIOD), -jnp.inf, F32)
        vecs.append(vec.at[:, BLOCK - sub_w:BLOCK + 1].set(bias_sub[:, ::-1]))
    vecs = jnp.stack(vecs, axis=1)
    vecs = vecs.reshape(N_HEAD_TILES, HEADS_PER_TILE, len(DILATED_CONFIGS), BIAS_PERIOD)
    return vecs.transpose(0, 2, 1, 3).reshape(N_HEAD_TILES, -1, BIAS_PERIOD)


def _sample_tables(rel_bias, wc, t_new):
    nk = wc + BLOCK
    tail = np.where(np.arange(BLOCK) < t_new, wc + np.arange(BLOCK), -1)
    key_pos = np.concatenate([np.arange(wc), tail])
    dist = wc + np.arange(t_new)[:, None] - key_pos[None, :]
    listed = (key_pos >= 0)[None, :] & (dist >= 0)
    cnt = sum((listed & (dist <= w) & (dist % d == 0)).astype(np.float32) for w, d in DILATED_CONFIGS)
    max_dist = wc + t_new - 1
    by_dist = rel_bias[_t5_bucket(jnp.arange(max_dist + 1))].T.astype(F32) * LOG2E
    rev = jnp.pad(by_dist[:, ::-1], ((0, 0), (0, nk)))
    bias = jnp.stack([rev[:, t_new - 1 - t:t_new - 1 - t + nk] for t in range(t_new)], axis=1)
    bias = jnp.where(jnp.asarray(cnt > 0)[None], bias, -jnp.inf)
    bias = bias.reshape(N_HEAD_TILES, HEADS_PER_TILE * t_new, nk)
    return bias, jnp.asarray(np.tile(cnt, (HEADS_PER_TILE, 1)))


def kernel(x_prompt, x_sample, cache_k, cache_v, state_pool, rel_bias, norm_mix, w_in, w_pool,
           pool_scale, w_out, norm_ffn, w_gate, w_up, w_down, norm_final):
    b, seq, d = x_prompt.shape
    db, t_new, _ = x_sample.shape
    depth = w_in.shape[0]
    wc = cache_k.shape[2]
    assert depth >= 1 and wc == MAX_WINDOW and seq >= MAX_WINDOW and t_new <= POOL_STATE

    tm_prompt = 512
    tm_in = b * seq // (2 * db)
    m_sample = db * t_new
    gf = norm_final.reshape(1, d)
    prompt_bias = _prompt_bias_vectors(rel_bias)
    sample_bias, sample_cnt = _sample_tables(rel_bias, wc, t_new)
    cache_kt, cache_vt = (c.transpose(0, 1, 3, 4, 2) for c in (cache_k, cache_v))
    zero_hist = jnp.zeros((POOL_HIST, POOL_WIDTH), F32)

    def to_tok_major(a):
        return a.transpose(1, 0, 2).reshape(a.shape[1] * db, a.shape[2])

    def to_seq_major(a):
        return a.reshape(a.shape[0] // db, db, a.shape[1]).transpose(1, 0, 2)

    xp = x_prompt.reshape(b * seq, d)
    xs = to_tok_major(x_sample)
    outs = {name: [] for name in ("kp", "vp", "pp", "ks", "vs", "ps")}
    for l in range(depth):
        g_mix = norm_mix[l].reshape(1, d)
        g_ffn = norm_ffn[l].reshape(1, d)
        w_in_l = w_in[l].astype(BF16)
        w_pool_l = w_pool[l].astype(BF16)
        scale_l = pool_scale[l].reshape(1, POOL_WIDTH)
        last = l == depth - 1

        hist = jnp.concatenate([jnp.zeros((db, POOL_WIDTH), F32), to_tok_major(state_pool[l])], axis=0)
        qs, ks, vs, _, _, pool_s, utail_s = _in_proj(
            xs, g_mix, w_in_l, hist, w_pool_l, scale_l, tm=m_sample, tiles_per_seq=1, shift=db,
            n_hist=POOL_STATE, tail_rows=m_sample)
        qs, ks, vs, us = (to_seq_major(a) for a in (qs, ks, vs, utail_s[0]))

        tiles_per_seq = seq // tm_in
        q, k, v, kt, vt, pool, utail, attn_s = _in_proj(
            xp, g_mix, w_in_l, zero_hist, w_pool_l, scale_l, tm=tm_in, tiles_per_seq=tiles_per_seq,
            shift=1, n_hist=0, tail_rows=POOL_HIST,
            sample=(qs, ks, vs, sample_bias, sample_cnt, cache_kt, cache_vt), layer=l)
        attn, w_out_l, w_gate_l, w_up_l, w_down_l = _attn_prompt(
            q.reshape(b, seq, -1), k.reshape(b, seq, -1), v.reshape(b, seq, -1), prompt_bias,
            w_out, w_gate, w_up, w_down, layer=l)
        xp = _out_proj(xp, attn.reshape(b * seq, -1), pool, w_out_l, tm=2 * tm_prompt)
        xp = _ffn(xp, g_ffn, w_gate_l, w_up_l, w_down_l, gf, tm=2 * tm_prompt, tf=512, final_norm=last)
        win = min(MAX_WINDOW, seq)
        outs["kp"].append(kt.transpose(0, 3, 1, 2)[:, -win:])
        outs["vp"].append(vt.transpose(0, 3, 1, 2)[:, -win:])
        outs["pp"].append(utail.reshape(b, tiles_per_seq, POOL_HIST, -1)[:, -1, -POOL_STATE:])

        xs = _out_proj(xs, to_tok_major(attn_s), pool_s, w_out_l, tm=m_sample)
        xs = _ffn(xs, g_ffn, w_gate_l, w_up_l, w_down_l, gf, tm=m_sample, tf=512, final_norm=last)
        outs["ks"].append(ks.reshape(db, t_new, N_HEADS, HEAD_DIM))
        outs["vs"].append(vs.reshape(db, t_new, N_HEADS, HEAD_DIM))
        outs["ps"].append(jnp.concatenate([state_pool[l], us], axis=1)[:, -POOL_STATE:])

    y_prompt = xp.reshape(b, seq, d)
    y_sample = to_seq_major(xs)
    return (y_prompt, y_sample, jnp.stack(outs["kp"]), jnp.stack(outs["vp"]), jnp.stack(outs["pp"]),
            jnp.stack(outs["ks"]), jnp.stack(outs["vs"]), jnp.stack(outs["ps"]))
```
